```python
import math
import jax, jax.numpy as jnp
from jax import lax
import numpy as np

D_MODEL = 1024
BATCH = 8
SEQ = 4096
DEPTH = 2

RG_WIDTH = D_MODEL
RG_BLOCKS = 8
RG_CONV = 4
RG_C = 8.0
SG_WIDTH = D_MODEL
SG_GROUPS = 8
SG_CHUNK = 128
ATT_HEADS = 8
HEAD_DIM = D_MODEL // ATT_HEADS
ATT_WIDTH = ATT_HEADS * HEAD_DIM
MOBA_BLOCK = 256
MOBA_TOPK = 3
Q_CHUNK = 16
D_FF = 3 * D_MODEL
FFN_CONV = 3
LN_EPS = 1e-5
DEEPNORM_ALPHA = (2 * DEPTH) ** 0.25
DEEPNORM_BETA = (8 * DEPTH) ** -0.25
NEG = -1e30

IN_WIDTHS = (RG_WIDTH, RG_WIDTH, SG_WIDTH, SG_WIDTH, ATT_WIDTH, ATT_WIDTH, ATT_WIDTH,
             D_MODEL, D_MODEL, D_MODEL)
D_IN = sum(IN_WIDTHS)
IN_SPLITS = tuple(sum(IN_WIDTHS[:i]) for i in range(1, len(IN_WIDTHS)))

kernel_name = "hybrid_rglru_gmlp_moba_deepnorm"


def layer_norm(x, g, b):
    xf = x.astype(jnp.float32)
    mu = xf.mean(-1, keepdims=True)
    var = jnp.square(xf - mu).mean(-1, keepdims=True)
    y = (xf - mu) * lax.rsqrt(var + LN_EPS)
    return (y * g + b).astype(x.dtype)


def causal_dwconv(x, w, b):
    K, C = w.shape
    y = lax.conv_general_dilated(
        x, w[:, None, :].astype(x.dtype), window_strides=(1,), padding=[(K - 1, 0)],
        dimension_numbers=("NWC", "WIO", "NWC"), feature_group_count=C)
    return y + b


def _lin_rec_combine(left, right):
    a_l, b_l = left
    a_r, b_r = right
    return a_r * a_l, a_r * b_l + b_r


def rg_lru(x, w_r, b_r, w_i, b_i, lam):
    B, S, C = x.shape
    xb = x.reshape(B, S, RG_BLOCKS, C // RG_BLOCKS)
    r = jax.nn.sigmoid(jnp.einsum("bsgi,gij->bsgj", xb, w_r).reshape(B, S, C) + b_r)
    i = jax.nn.sigmoid(jnp.einsum("bsgi,gij->bsgj", xb, w_i).reshape(B, S, C) + b_i)
    log_a = -RG_C * r.astype(jnp.float32) * jax.nn.softplus(-lam.astype(jnp.float32))
    a = jnp.exp(log_a)
    u = jnp.sqrt(-jnp.expm1(2.0 * log_a)) * (i * x).astype(jnp.float32)
    _, h = lax.associative_scan(_lin_rec_combine, (a, u), axis=1)
    return h.astype(x.dtype)


def spatial_gating(u, v, ln_g, ln_b, w_s, b_s):
    B, S, C = v.shape
    v = layer_norm(v, ln_g, ln_b)
    vr = v.reshape(B, S // SG_CHUNK, SG_CHUNK, SG_GROUPS, C // SG_GROUPS)
    causal = jnp.tril(jnp.ones((SG_CHUNK, SG_CHUNK), dtype=bool))
    w = jnp.where(causal[None], w_s, 0)
    mixed = jnp.einsum("gts,bnsgc->bntgc", w, vr) + b_s.T[None, None, :, :, None]
    return u * mixed.reshape(B, S, C)


def moba_attention(q, k, v):
    B, S, H, D = q.shape
    nb = -(-S // MOBA_BLOCK)
    n_sel = min(MOBA_TOPK, nb)
    pad = nb * MOBA_BLOCK - S
    kp = jnp.pad(k, ((0, 0), (0, pad), (0, 0), (0, 0)))
    vp = jnp.pad(v, ((0, 0), (0, pad), (0, 0), (0, 0)))
    kb = kp.reshape(B, nb, MOBA_BLOCK, H, D).transpose(0, 3, 1, 2, 4)
    vb = vp.reshape(B, nb, MOBA_BLOCK, H, D).transpose(0, 3, 1, 2, 4)
    k_mean = kb.mean(axis=3)
    scale = D ** -0.5
    n_chunks = S // Q_CHUNK
    qc = q.reshape(B, n_chunks, Q_CHUNK, H, D).transpose(1, 0, 3, 2, 4)
    b_ix = jnp.arange(B)[:, None, None, None]
    h_ix = jnp.arange(H)[None, :, None, None]
    blk_ids = jnp.arange(nb)
    LK = n_sel * MOBA_BLOCK

    def chunk(args):
        c, qb = args
        q0 = c * Q_CHUNK
        own = q0 // MOBA_BLOCK
        gate = jnp.einsum("bhqd,bhjd->bhqj", qb, k_mean).astype(jnp.float32)
        gate = jnp.where(blk_ids < own, gate, -jnp.inf)
        _, sel = lax.top_k(gate, n_sel)
        sel_valid = sel < own
        k_sel = kb[b_ix, h_ix, sel]
        v_sel = vb[b_ix, h_ix, sel]
        s_sel = jnp.einsum("bhqd,bhqmld->bhqml", qb, k_sel).astype(jnp.float32) * scale
        s_sel = jnp.where(sel_valid[..., None], s_sel, NEG)
        k_own = lax.dynamic_index_in_dim(kb, own, axis=2, keepdims=False)
        v_own = lax.dynamic_index_in_dim(vb, own, axis=2, keepdims=False)
        s_own = jnp.einsum("bhqd,bhld->bhql", qb, k_own).astype(jnp.float32) * scale
        q_pos = q0 + jnp.arange(Q_CHUNK)
        k_pos = own * MOBA_BLOCK + jnp.arange(MOBA_BLOCK)
        s_own = jnp.where(k_pos[None, :] <= q_pos[:, None], s_own, NEG)
        logits = jnp.concatenate([s_sel.reshape(B, H, Q_CHUNK, LK), s_own], axis=-1)
        p = jax.nn.softmax(logits, axis=-1).astype(qb.dtype)
        p_sel = p[..., :LK].reshape(B, H, Q_CHUNK, n_sel, MOBA_BLOCK)
        p_own = p[..., LK:]
        return (jnp.einsum("bhqml,bhqmld->bhqd", p_sel, v_sel)
                + jnp.einsum("bhql,bhld->bhqd", p_own, v_own))

    out = lax.map(chunk, (jnp.arange(n_chunks), qc))
    return out.transpose(1, 0, 3, 2, 4).reshape(B, S, H * D)


def hybrid_mixer(x, w_in, conv_w, conv_b, w_r, b_r, w_i, b_i, lam,
                 sg_g, sg_b, w_s, b_s, w_out):
    B, S, _ = x.shape
    proj = x @ w_in
    a_x, a_gate, s_u, s_v, q, k, v, g_a, g_s, g_m = jnp.split(proj, IN_SPLITS, axis=-1)
    y_a = rg_lru(causal_dwconv(a_x, conv_w, conv_b), w_r, b_r, w_i, b_i, lam) * jax.nn.gelu(a_gate)
    y_s = spatial_gating(jax.nn.gelu(s_u), jax.nn.gelu(s_v), sg_g, sg_b, w_s, b_s)
    hs = (B, S, ATT_HEADS, HEAD_DIM)
    y_m = moba_attention(q.reshape(hs), k.reshape(hs), v.reshape(hs))
    merged = (jax.nn.sigmoid(g_a) * y_a + jax.nn.sigmoid(g_s) * y_s
              + jax.nn.sigmoid(g_m) * y_m)
    return merged @ w_out


def conv_ffn(x, w_up, conv_w, conv_b, w_down):
    h = x @ w_up
    h_gate, h_up = h[..., :D_FF], h[..., D_FF:]
    h_gate = causal_dwconv(h_gate, conv_w, conv_b)
    return (jax.nn.gelu(h_gate) * h_up) @ w_down


def setup_inputs(seed: int = 0) -> dict:
    key = jax.random.key(seed)
    ks = jax.random.split(key, 24)
    L = DEPTH

    def nrm(k, shape, scale):
        return jax.random.normal(k, shape, jnp.float32) * scale

    bw = RG_WIDTH // RG_BLOCKS
    a_c = jax.random.uniform(ks[8], (L, RG_WIDTH), jnp.float32, minval=0.9, maxval=0.999)
    s = a_c ** (1.0 / RG_C)
    return {
        "x": nrm(ks[0], (BATCH, SEQ, D_MODEL), 1.0),
        "w_in": nrm(ks[1], (L, D_MODEL, D_IN), D_MODEL ** -0.5),
        "conv_rg_w": nrm(ks[2], (L, RG_CONV, RG_WIDTH), RG_CONV ** -0.5),
        "conv_rg_b": nrm(ks[3], (L, RG_WIDTH), 0.01),
        "w_rgate": nrm(ks[4], (L, RG_BLOCKS, bw, bw), bw ** -0.5),
        "b_rgate": nrm(ks[5], (L, RG_WIDTH), 0.01),
        "w_igate": nrm(ks[6], (L, RG_BLOCKS, bw, bw), bw ** -0.5),
        "b_igate": nrm(ks[7], (L, RG_WIDTH), 0.01),
        "lru_lambda": jnp.log(s) - jnp.log1p(-s),
        "sgu_ln_g": 1.0 + nrm(ks[9], (L, SG_WIDTH), 0.01),
        "sgu_ln_b": nrm(ks[10], (L, SG_WIDTH), 0.01),
        "w_spatial": nrm(ks[11], (L, SG_GROUPS, SG_CHUNK, SG_CHUNK), SG_CHUNK ** -0.5),
        "b_spatial": 1.0 + nrm(ks[12], (L, SG_GROUPS, SG_CHUNK), 0.01),
        "w_out": nrm(ks[13], (L, D_MODEL, D_MODEL), D_MODEL ** -0.5 * DEEPNORM_BETA),
        "ln_mix_g": 1.0 + nrm(ks[14], (L, D_MODEL), 0.01),
        "ln_mix_b": nrm(ks[15], (L, D_MODEL), 0.01),
        "w_ffn_up": nrm(ks[16], (L, D_MODEL, 2 * D_FF), D_MODEL ** -0.5),
        "conv_ffn_w": nrm(ks[17], (L, FFN_CONV, D_FF), FFN_CONV ** -0.5),
        "conv_ffn_b": nrm(ks[18], (L, D_FF), 0.01),
        "w_ffn_down": nrm(ks[19], (L, D_FF, D_MODEL), D_FF ** -0.5 * DEEPNORM_BETA),
        "ln_ffn_g": 1.0 + nrm(ks[20], (L, D_MODEL), 0.01),
        "ln_ffn_b": nrm(ks[21], (L, D_MODEL), 0.01),
    }


def reference(x, w_in, conv_rg_w, conv_rg_b, w_rgate, b_rgate, w_igate, b_igate,
              lru_lambda, sgu_ln_g, sgu_ln_b, w_spatial, b_spatial, w_out,
              ln_mix_g, ln_mix_b, w_ffn_up, conv_ffn_w, conv_ffn_b, w_ffn_down,
              ln_ffn_g, ln_ffn_b):
    for l in range(DEPTH):
        mix = hybrid_mixer(x, w_in[l], conv_rg_w[l], conv_rg_b[l], w_rgate[l], b_rgate[l],
                           w_igate[l], b_igate[l], lru_lambda[l], sgu_ln_g[l], sgu_ln_b[l],
                           w_spatial[l], b_spatial[l], w_out[l])
        x = layer_norm(DEEPNORM_ALPHA * x + mix, ln_mix_g[l], ln_mix_b[l])
        ff = conv_ffn(x, w_ffn_up[l], conv_ffn_w[l], conv_ffn_b[l], w_ffn_down[l])
        x = layer_norm(DEEPNORM_ALPHA * x + ff, ln_ffn_g[l], ln_ffn_b[l])
    return x
```

```python
import functools
import math

import jax
import jax.numpy as jnp
from jax import lax
from jax.experimental import pallas as pl
from jax.experimental.pallas import tpu as pltpu

F32 = jnp.float32
BF16 = jnp.bfloat16

RG_BLOCKS = 8
RG_CONV = 4
RG_C = 8.0
SG_GROUPS = 8
SG_CHUNK = 128
ATT_HEADS = 8
HEAD_DIM = 128
MOBA_BLOCK = 256
MOBA_TOPK = 3
FFN_CONV = 3
LN_EPS = 1e-5
DEPTH = 2
DEEPNORM_ALPHA = (2 * DEPTH) ** 0.25
NEG = -1e30

LANES = 128
SUBLANES = 8
VMEM_LIMIT = 56 * 1024 * 1024

_SQRT_2_OVER_PI = math.sqrt(2.0 / math.pi)


def _gelu(x):
    return x * (0.5 * (1.0 + jnp.tanh(_SQRT_2_OVER_PI * (x + 0.044715 * (x * x * x)))))


def _sigmoid(x):
    return 1.0 / (1.0 + jnp.exp(-x))


def _layer_norm(y, g, b):
    mu = jnp.mean(y, axis=-1, keepdims=True)
    d = y - mu
    var = jnp.mean(d * d, axis=-1, keepdims=True)
    return d * lax.rsqrt(var + LN_EPS) * g + b


def _params(*sem):
    return pltpu.CompilerParams(dimension_semantics=sem, vmem_limit_bytes=VMEM_LIMIT)


def _in_proj_kernel(x_ref, w_ref, o_ref):
    o_ref[...] = jnp.dot(x_ref[...].astype(BF16), w_ref[...],
                         preferred_element_type=F32).astype(o_ref.dtype)


def _in_proj(x2, w_bf, tm=1024, tn=1024):
    n, d = x2.shape
    d_in = w_bf.shape[1]
    return pl.pallas_call(
        _in_proj_kernel,
        grid=(n // tm, d_in // tn),
        in_specs=[pl.BlockSpec((tm, d), lambda i, j: (i, 0)),
                  pl.BlockSpec((d, tn), lambda i, j: (0, j))],
        out_specs=pl.BlockSpec((tm, tn), lambda i, j: (i, j)),
        out_shape=jax.ShapeDtypeStruct((n, d_in), BF16),
        compiler_params=_params("parallel", "arbitrary"),
        name="in_proj",
    )(x2, w_bf)


def _rglru_kernel(ax_ref, ag_ref, ga_ref, cw_ref, cb_ref, wr_ref, br_ref, wi_ref, bi_ref,
                  lam_ref, o_ref, xbuf, a_sc, u_sc, hcarry, *, nb, t, cb, sp):
    i = pl.program_id(1)
    halo = SUBLANES

    @pl.when(i == 0)
    def _():
        xbuf[:, 0:halo, :] = jnp.zeros((nb, halo, cb), F32)
        hcarry[...] = jnp.zeros_like(hcarry)

    lam = lam_ref[...]
    neg_lam = -lam
    softplus = jnp.maximum(neg_lam, 0.0) + jnp.log1p(jnp.exp(-jnp.abs(neg_lam)))
    decay_rate = -RG_C * softplus
    cw = cw_ref[...]
    cbias = cb_ref[...]
    nsub = cb // LANES

    def gates(b, carry):
        x = ax_ref[b].astype(F32)
        xbuf[b, halo:halo + t, :] = x
        xc = cw[3:4] * x + cbias
        for j in range(1, RG_CONV):
            xc = xc + cw[3 - j:4 - j] * xbuf[b, halo - j:halo - j + t, :]
        xbuf[b, 0:halo, :] = xbuf[b, t:t + halo, :]
        xcb = xc.astype(BF16)
        r_parts, i_parts = [], []
        for s in range(nsub):
            xs = xcb[:, s * LANES:(s + 1) * LANES]
            r_parts.append(jnp.dot(xs, wr_ref[s].astype(BF16), preferred_element_type=F32))
            i_parts.append(jnp.dot(xs, wi_ref[s].astype(BF16), preferred_element_type=F32))
        r = _sigmoid(jnp.concatenate(r_parts, axis=-1) + br_ref[...])
        ig = _sigmoid(jnp.concatenate(i_parts, axis=-1) + bi_ref[...])
        log_a = decay_rate * r
        a = jnp.exp(log_a)
        u = jnp.sqrt(1.0 - a * a) * (ig * xc)
        row0 = pl.multiple_of(b * sp, SUBLANES)
        for s in range(nsub):
            a_sc[s, pl.ds(row0, t), :] = a[:, s * LANES:(s + 1) * LANES]
            u_sc[s, pl.ds(row0, t), :] = u[:, s * LANES:(s + 1) * LANES]
        return carry

    lax.fori_loop(0, nb, gates, 0)

    def step(p, hs):
        new = []
        for s in range(nsub):
            rows = pl.ds(p, nb, stride=sp)
            h = a_sc[s, rows, :] * hs[s] + u_sc[s, rows, :]
            u_sc[s, rows, :] = h
            new.append(h)
        return tuple(new)

    hs = lax.fori_loop(0, t, step, tuple(hcarry[s] for s in range(nsub)), unroll=8)
    for s in range(nsub):
        hcarry[s] = hs[s]

    def emit(b, carry):
        row0 = pl.multiple_of(b * sp, SUBLANES)
        h = jnp.concatenate([u_sc[s, pl.ds(row0, t), :] for s in range(nsub)], axis=-1)
        y = h * _gelu(ag_ref[b].astype(F32)) * _sigmoid(ga_ref[b].astype(F32))
        o_ref[b] = y.astype(o_ref.dtype)
        return carry

    lax.fori_loop(0, nb, emit, 0)


def _rglru(proj3, conv_w, conv_b, w_r, b_r, w_i, b_i, lam, d, t=256, cb=512):
    nb, s, _ = proj3.shape
    ncb = d // cb
    nsub = cb // LANES
    sp = t + SUBLANES
    row = lambda a: a.reshape(1, d)
    kern = functools.partial(_rglru_kernel, nb=nb, t=t, cb=cb, sp=sp)
    col = lambda off: (lambda c, i: (0, i, off * ncb + c))
    vec = pl.BlockSpec((1, cb), lambda c, i: (0, c))
    wspec = pl.BlockSpec((nsub, LANES, LANES), lambda c, i: (c, 0, 0))
    return pl.pallas_call(
        kern,
        grid=(ncb, s // t),
        in_specs=[pl.BlockSpec((nb, t, cb), col(0)),
                  pl.BlockSpec((nb, t, cb), col(1)),
                  pl.BlockSpec((nb, t, cb), col(7)),
                  pl.BlockSpec((RG_CONV, cb), lambda c, i: (0, c)),
                  vec, wspec, vec, wspec, vec, vec],
        out_specs=pl.BlockSpec((nb, t, cb), lambda c, i: (0, i, c)),
        out_shape=jax.ShapeDtypeStruct((nb, s, d), BF16),
        scratch_shapes=[pltpu.VMEM((nb, t + SUBLANES, cb), F32),
                        pltpu.VMEM((nsub, nb * sp, LANES), F32),
                        pltpu.VMEM((nsub, nb * sp, LANES), F32),
                        pltpu.VMEM((nsub, nb, LANES), F32)],
        compiler_params=_params("parallel", "arbitrary"),
        name="rglru",
    )(proj3, proj3, proj3, conv_w, row(conv_b), w_r, row(b_r), w_i, row(b_i), row(lam))


def _sgu_kernel(su_ref, sv_ref, gs_ref, lg_ref, lb_ref, ws_ref, bias_ref, o_ref, *, rows):
    v = _layer_norm(_gelu(sv_ref[...].astype(F32)), lg_ref[...], lb_ref[...]).astype(BF16)
    tri = (lax.broadcasted_iota(jnp.int32, (SG_CHUNK, SG_CHUNK), 1)
           <= lax.broadcasted_iota(jnp.int32, (SG_CHUNK, SG_CHUNK), 0))
    for g in range(SG_GROUPS):
        w = jnp.where(tri, ws_ref[g], 0.0).astype(BF16)
        cols = slice(g * LANES, (g + 1) * LANES)
        for c in range(rows // SG_CHUNK):
            rs = slice(c * SG_CHUNK, (c + 1) * SG_CHUNK)
            mixed = jnp.dot(w, v[rs, cols], preferred_element_type=F32) + bias_ref[:, cols]
            u = _gelu(su_ref[rs, cols].astype(F32))
            gate = _sigmoid(gs_ref[rs, cols].astype(F32))
            o_ref[rs, cols] = (u * mixed * gate).astype(o_ref.dtype)


def _sgu(proj2, ln_g, ln_b, w_s, b_s, d, rows=512):
    n = proj2.shape[0]
    bias = jnp.repeat(b_s.T, d // SG_GROUPS, axis=1)
    col = lambda off: (lambda r: (r, off))
    full = lambda shape: pl.BlockSpec(shape, lambda r: (0,) * len(shape))
    return pl.pallas_call(
        functools.partial(_sgu_kernel, rows=rows),
        grid=(n // rows,),
        in_specs=[pl.BlockSpec((rows, d), col(2)),
                  pl.BlockSpec((rows, d), col(3)),
                  pl.BlockSpec((rows, d), col(8)),
                  full((1, d)), full((1, d)),
                  full((SG_GROUPS, SG_CHUNK, SG_CHUNK)), full((SG_CHUNK, d))],
        out_specs=pl.BlockSpec((rows, d), lambda r: (r, 0)),
        out_shape=jax.ShapeDtypeStruct((n, d), BF16),
        compiler_params=_params("parallel"),
        name="sgu",
    )(proj2, proj2, proj2, ln_g.reshape(1, d), ln_b.reshape(1, d), w_s, bias)


def _moba_kernel(q_ref, k_ref, v_ref, gm_ref, o_ref, kmean, *, seq):
    qi = pl.program_id(2)
    blk = MOBA_BLOCK
    nblk = seq // blk
    scale = HEAD_DIM ** -0.5

    @pl.when(qi == 0)
    def _():
        row = lax.broadcasted_iota(jnp.int32, (LANES, seq), 0)
        pos = lax.broadcasted_iota(jnp.int32, (LANES, seq), 1)
        ind = jnp.where((pos >= row * blk) & (pos < (row + 1) * blk), 1.0, 0.0).astype(BF16)
        sums = jnp.dot(ind, k_ref[...], preferred_element_type=F32)
        kmean[...] = (sums * (1.0 / blk)).astype(BF16)

    q = q_ref[...]
    contract_last = (((1,), (1,)), ((), ()))
    gate = lax.dot_general(q, kmean[...], contract_last, preferred_element_type=F32)
    lane = lax.broadcasted_iota(jnp.int32, (blk, LANES), 1)
    g = jnp.where(lane < qi, gate, -jnp.inf)
    sel = jnp.zeros((blk, LANES), jnp.int32)
    for _ in range(min(MOBA_TOPK, nblk)):
        m = jnp.max(g, axis=-1, keepdims=True)
        idx = jnp.min(jnp.where(g == m, lane, LANES), axis=-1, keepdims=True)
        pick = lane == idx
        sel = jnp.where(pick, 1, sel)
        g = jnp.where(pick, -jnp.inf, g)
    sel = jnp.where(lane < qi, sel, 0)

    own0 = pl.multiple_of(qi * blk, blk)
    s = lax.dot_general(q, k_ref[pl.ds(own0, blk), :], contract_last,
                        preferred_element_type=F32) * scale
    qpos = lax.broadcasted_iota(jnp.int32, (blk, blk), 0)
    kpos = lax.broadcasted_iota(jnp.int32, (blk, blk), 1)
    s = jnp.where(kpos <= qpos, s, NEG)
    m0 = jnp.max(s, axis=-1, keepdims=True)
    p = jnp.exp(s - m0)
    l0 = jnp.sum(p, axis=-1, keepdims=True)
    acc0 = jnp.dot(p.astype(BF16), v_ref[pl.ds(own0, blk), :], preferred_element_type=F32)

    def body(j, carry):
        m_prev, l_prev, acc = carry
        j0 = pl.multiple_of(j * blk, blk)
        sj = lax.dot_general(q, k_ref[pl.ds(j0, blk), :], contract_last,
                             preferred_element_type=F32) * scale
        chosen = jnp.max(jnp.where(lane == j, sel, 0), axis=-1, keepdims=True)
        sj = jnp.where(chosen > 0, sj, NEG)
        m_new = jnp.maximum(m_prev, jnp.max(sj, axis=-1, keepdims=True))
        alpha = jnp.exp(m_prev - m_new)
        pj = jnp.exp(sj - m_new)
        l_new = alpha * l_prev + jnp.sum(pj, axis=-1, keepdims=True)
        acc = alpha * acc + jnp.dot(pj.astype(BF16), v_ref[pl.ds(j0, blk), :],
                                    preferred_element_type=F32)
        return m_new, l_new, acc

    _, l_fin, acc = lax.fori_loop(0, qi, body, (m0, l0, acc0))
    y = acc / l_fin
    o_ref[...] = (y * _sigmoid(gm_ref[...].astype(F32))).astype(o_ref.dtype)


def _moba(proj2, nb, seq, d):
    n = proj2.shape[0]
    blk = MOBA_BLOCK
    nq = seq // blk
    hpd = d // HEAD_DIM
    return pl.pallas_call(
        functools.partial(_moba_kernel, seq=seq),
        grid=(nb, ATT_HEADS, nq),
        in_specs=[pl.BlockSpec((blk, HEAD_DIM), lambda b, h, i: (b * nq + i, 4 * hpd + h)),
                  pl.BlockSpec((seq, HEAD_DIM), lambda b, h, i: (b, 5 * hpd + h)),
                  pl.BlockSpec((seq, HEAD_DIM), lambda b, h, i: (b, 6 * hpd + h)),
                  pl.BlockSpec((blk, HEAD_DIM), lambda b, h, i: (b * nq + i, 9 * hpd + h))],
        out_specs=pl.BlockSpec((blk, HEAD_DIM), lambda b, h, i: (b * nq + i, h)),
        out_shape=jax.ShapeDtypeStruct((n, d), BF16),
        scratch_shapes=[pltpu.VMEM((LANES, HEAD_DIM), BF16)],
        compiler_params=_params("parallel", "parallel", "arbitrary"),
        name="moba",
    )(proj2, proj2, proj2, proj2)


def _out_proj_kernel(ya_ref, ys_ref, ym_ref, x_ref, w_ref, g_ref, b_ref, o_ref):
    merged = (ya_ref[...].astype(F32) + ys_ref[...].astype(F32) + ym_ref[...].astype(F32))
    mix = jnp.dot(merged.astype(BF16), w_ref[...], preferred_element_type=F32)
    o_ref[...] = _layer_norm(DEEPNORM_ALPHA * x_ref[...] + mix, g_ref[...], b_ref[...])


def _out_proj(ya, ys, ym, x2, w_bf, g, b, tm=512):
    n, d = x2.shape
    rowblk = pl.BlockSpec((tm, d), lambda i: (i, 0))
    full = lambda shape: pl.BlockSpec(shape, lambda i: (0,) * len(shape))
    return pl.pallas_call(
        _out_proj_kernel,
        grid=(n // tm,),
        in_specs=[rowblk, rowblk, rowblk, rowblk, full((d, d)), full((1, d)), full((1, d))],
        out_specs=rowblk,
        out_shape=jax.ShapeDtypeStruct((n, d), F32),
        compiler_params=_params("parallel"),
        name="out_proj",
    )(ya, ys, ym, x2, w_bf, g.reshape(1, d), b.reshape(1, d))


def _ffn_kernel(x_ref, xh_ref, wup_ref, cw_ref, cb_ref, wdn_ref, g_ref, b_ref, o_ref,
                *, tm, tf, d_ff, seq):
    i = pl.program_id(0)
    halo = SUBLANES
    x = x_ref[...]
    xb = x.astype(BF16)
    seq_start = (i * tm) % seq == 0
    xh = jnp.where(seq_start, 0.0, xh_ref[...]).astype(BF16)
    acc = jnp.zeros(x.shape, F32)
    for c in range(d_ff // tf):
        cols = slice(c * tf, (c + 1) * tf)
        wg = wup_ref[:, cols]
        hg = jnp.dot(xb, wg, preferred_element_type=F32)
        hh = jnp.dot(xh, wg, preferred_element_type=F32)
        hu = jnp.dot(xb, wup_ref[:, d_ff + c * tf:d_ff + (c + 1) * tf],
                     preferred_element_type=F32)
        ext = jnp.concatenate([hh, hg], axis=0)
        cw = cw_ref[:, cols]
        gate = cw[2:3] * hg + cb_ref[:, cols]
        for j in range(1, FFN_CONV):
            gate = gate + cw[2 - j:3 - j] * ext[halo - j:halo - j + tm, :]
        act = (_gelu(gate) * hu).astype(BF16)
        acc = acc + jnp.dot(act, wdn_ref[cols, :], preferred_element_type=F32)
    o_ref[...] = _layer_norm(DEEPNORM_ALPHA * x + acc, g_ref[...], b_ref[...])


def _ffn(x2, w_up_bf, conv_w, conv_b, w_dn_bf, g, b, seq, tm=512, tf=512):
    n, d = x2.shape
    d_ff = w_dn_bf.shape[0]
    hb = tm // SUBLANES
    full = lambda shape: pl.BlockSpec(shape, lambda i: (0,) * len(shape),
                                      pipeline_mode=pl.Buffered(1))
    return pl.pallas_call(
        functools.partial(_ffn_kernel, tm=tm, tf=tf, d_ff=d_ff, seq=seq),
        grid=(n // tm,),
        in_specs=[pl.BlockSpec((tm, d), lambda i: (i, 0)),
                  pl.BlockSpec((SUBLANES, d), lambda i: (jnp.maximum(i * hb - 1, 0), 0)),
                  full((d, 2 * d_ff)), full((FFN_CONV, d_ff)), full((1, d_ff)),
                  full((d_ff, d)), full((1, d)), full((1, d))],
        out_specs=pl.BlockSpec((tm, d), lambda i: (i, 0)),
        out_shape=jax.ShapeDtypeStruct((n, d), F32),
        compiler_params=_params("parallel"),
        name="ffn",
    )(x2, x2, w_up_bf, conv_w, conv_b.reshape(1, d_ff), w_dn_bf, g.reshape(1, d), b.reshape(1, d))


def kernel(x, w_in, conv_rg_w, conv_rg_b, w_rgate, b_rgate, w_igate, b_igate, lru_lambda, sgu_ln_g, sgu_ln_b, w_spatial, b_spatial, w_out, ln_mix_g, ln_mix_b, w_ffn_up, conv_ffn_w, conv_ffn_b, w_ffn_down, ln_ffn_g, ln_ffn_b):
    nb, seq, d = x.shape
    n = nb * seq
    depth = w_in.shape[0]
    x2 = x.reshape(n, d)
    for l in range(depth):
        proj = _in_proj(x2, w_in[l].astype(BF16))
        ya = _rglru(proj.reshape(nb, seq, -1), conv_rg_w[l], conv_rg_b[l], w_rgate[l], b_rgate[l],
                    w_igate[l], b_igate[l], lru_lambda[l], d).reshape(n, d)
        ys = _sgu(proj, sgu_ln_g[l], sgu_ln_b[l], w_spatial[l], b_spatial[l], d)
        ym = _moba(proj, nb, seq, d)
        x2 = _out_proj(ya, ys, ym, x2, w_out[l].astype(BF16), ln_mix_g[l], ln_mix_b[l])
        x2 = _ffn(x2, w_ffn_up[l].astype(BF16), conv_ffn_w[l], conv_ffn_b[l],
                  w_ffn_down[l].astype(BF16), ln_ffn_g[l], ln_ffn_b[l], seq)
    return x2.reshape(nb, seq, d)
```

```python
import functools
import math

import jax
import jax.numpy as jnp
from jax import lax
from jax.experimental import pallas as pl
from jax.experimental.pallas import tpu as pltpu

F32 = jnp.float32
BF16 = jnp.bfloat16

RG_BLOCKS = 8
RG_CONV = 4
RG_C = 8.0
SG_GROUPS = 8
SG_CHUNK = 128
ATT_HEADS = 8
HEAD_DIM = 128
MOBA_BLOCK = 256
MOBA_TOPK = 3
FFN_CONV = 3
LN_EPS = 1e-5
DEPTH = 2
DEEPNORM_ALPHA = (2 * DEPTH) ** 0.25
NEG = -1e30

LANES = 128
SUBLANES = 8
VMEM_LIMIT = 56 * 1024 * 1024

_SQRT_2_OVER_PI = math.sqrt(2.0 / math.pi)


def _gelu(x):
    return x * (0.5 * (1.0 + jnp.tanh(_SQRT_2_OVER_PI * (x + 0.044715 * (x * x * x)))))


def _sigmoid(x):
    return 1.0 / (1.0 + jnp.exp(-x))


def _layer_norm(y, g, b):
    mu = jnp.mean(y, axis=-1, keepdims=True)
    d = y - mu
    var = jnp.mean(d * d, axis=-1, keepdims=True)
    return d * lax.rsqrt(var + LN_EPS) * g + b


def _params(*sem):
    return pltpu.CompilerParams(dimension_semantics=sem, vmem_limit_bytes=VMEM_LIMIT)


def _in_proj_kernel(x_ref, w_ref, o_ref):
    o_ref[...] = jnp.dot(x_ref[...].astype(BF16), w_ref[...],
                         preferred_element_type=F32).astype(o_ref.dtype)


def _in_proj(x2, w_bf, tm=1024, tn=1024):
    n, d = x2.shape
    d_in = w_bf.shape[1]
    return pl.pallas_call(
        _in_proj_kernel,
        grid=(n // tm, d_in // tn),
        in_specs=[pl.BlockSpec((tm, d), lambda i, j: (i, 0)),
                  pl.BlockSpec((d, tn), lambda i, j: (0, j))],
        out_specs=pl.BlockSpec((tm, tn), lambda i, j: (i, j)),
        out_shape=jax.ShapeDtypeStruct((n, d_in), BF16),
        compiler_params=_params("parallel", "arbitrary"),
        name="in_proj",
    )(x2, w_bf)


def _rglru_kernel(ax_ref, ag_ref, ga_ref, cw_ref, cb_ref, wr_ref, br_ref, wi_ref, bi_ref,
                  lam_ref, o_ref, xbuf, a_sc, u_sc, hcarry, *, nb, t, cb, sp):
    i = pl.program_id(1)
    halo = SUBLANES

    @pl.when(i == 0)
    def _():
        xbuf[:, 0:halo, :] = jnp.zeros((nb, halo, cb), F32)
        hcarry[...] = jnp.zeros_like(hcarry)

    lam = lam_ref[...]
    neg_lam = -lam
    softplus = jnp.maximum(neg_lam, 0.0) + jnp.log1p(jnp.exp(-jnp.abs(neg_lam)))
    decay_rate = -RG_C * softplus
    cw = cw_ref[...]
    cbias = cb_ref[...]
    nsub = cb // LANES

    def gates(b, carry):
        x = ax_ref[b].astype(F32)
        xbuf[b, halo:halo + t, :] = x
        xc = cw[3:4] * x + cbias
        for j in range(1, RG_CONV):
            xc = xc + cw[3 - j:4 - j] * xbuf[b, halo - j:halo - j + t, :]
        xbuf[b, 0:halo, :] = xbuf[b, t:t + halo, :]
        xcb = xc.astype(BF16)
        r_parts, i_parts = [], []
        for s in range(nsub):
            xs = xcb[:, s * LANES:(s + 1) * LANES]
            r_parts.append(jnp.dot(xs, wr_ref[s].astype(BF16), preferred_element_type=F32))
            i_parts.append(jnp.dot(xs, wi_ref[s].astype(BF16), preferred_element_type=F32))
        r = _sigmoid(jnp.concatenate(r_parts, axis=-1) + br_ref[...])
        ig = _sigmoid(jnp.concatenate(i_parts, axis=-1) + bi_ref[...])
        log_a = decay_rate * r
        a = jnp.exp(log_a)
        u = jnp.sqrt(1.0 - a * a) * (ig * xc)
        row0 = pl.multiple_of(b * sp, SUBLANES)
        for s in range(nsub):
            a_sc[s, pl.ds(row0, t), :] = a[:, s * LANES:(s + 1) * LANES]
            u_sc[s, pl.ds(row0, t), :] = u[:, s * LANES:(s + 1) * LANES]
        return carry

    lax.fori_loop(0, nb, gates, 0)

    def step(p, hs):
        new = []
        for s in range(nsub):
            rows = pl.ds(p, nb, stride=sp)
            h = a_sc[s, rows, :] * hs[s] + u_sc[s, rows, :]
            u_sc[s, rows, :] = h
            new.append(h)
        return tuple(new)

    hs = lax.fori_loop(0, t, step, tuple(hcarry[s] for s in range(nsub)), unroll=8)
    for s in range(nsub):
        hcarry[s] = hs[s]

    def emit(b, carry):
        row0 = pl.multiple_of(b * sp, SUBLANES)
        h = jnp.concatenate([u_sc[s, pl.ds(row0, t), :] for s in range(nsub)], axis=-1)
        y = h * _gelu(ag_ref[b].astype(F32)) * _sigmoid(ga_ref[b].astype(F32))
        o_ref[b] = y.astype(o_ref.dtype)
        return carry

    lax.fori_loop(0, nb, emit, 0)


def _rglru(proj3, conv_w, conv_b, w_r, b_r, w_i, b_i, lam, d, t=256, cb=512):
    nb, s, _ = proj3.shape
    ncb = d // cb
    nsub = cb // LANES
    sp = t + SUBLANES
    row = lambda a: a.reshape(1, d)
    kern = functools.partial(_rglru_kernel, nb=nb, t=t, cb=cb, sp=sp)
    col = lambda off: (lambda c, i: (0, i, off * ncb + c))
    vec = pl.BlockSpec((1, cb), lambda c, i: (0, c))
    wspec = pl.BlockSpec((nsub, LANES, LANES), lambda c, i: (c, 0, 0))
    return pl.pallas_call(
        kern,
        grid=(ncb, s // t),
        in_specs=[pl.BlockSpec((nb, t, cb), col(0)),
                  pl.BlockSpec((nb, t, cb), col(1)),
                  pl.BlockSpec((nb, t, cb), col(7)),
                  pl.BlockSpec((RG_CONV, cb), lambda c, i: (0, c)),
                  vec, wspec, vec, wspec, vec, vec],
        out_specs=pl.BlockSpec((nb, t, cb), lambda c, i: (0, i, c)),
        out_shape=jax.ShapeDtypeStruct((nb, s, d), BF16),
        scratch_shapes=[pltpu.VMEM((nb, t + SUBLANES, cb), F32),
                        pltpu.VMEM((nsub, nb * sp, LANES), F32),
                        pltpu.VMEM((nsub, nb * sp, LANES), F32),
                        pltpu.VMEM((nsub, nb, LANES), F32)],
        compiler_params=_params("parallel", "arbitrary"),
        name="rglru",
    )(proj3, proj3, proj3, conv_w, row(conv_b), w_r, row(b_r), w_i, row(b_i), row(lam))


def _sgu_kernel(su_ref, sv_ref, gs_ref, lg_ref, lb_ref, ws_ref, bias_ref, o_ref, *, rows):
    v = _layer_norm(_gelu(sv_ref[...].astype(F32)), lg_ref[...], lb_ref[...]).astype(BF16)
    tri = (lax.broadcasted_iota(jnp.int32, (SG_CHUNK, SG_CHUNK), 1)
           <= lax.broadcasted_iota(jnp.int32, (SG_CHUNK, SG_CHUNK), 0))
    for g in range(SG_GROUPS):
        w = jnp.where(tri, ws_ref[g], 0.0).astype(BF16)
        cols = slice(g * LANES, (g + 1) * LANES)
        for c in range(rows // SG_CHUNK):
            rs = slice(c * SG_CHUNK, (c + 1) * SG_CHUNK)
            mixed = jnp.dot(w, v[rs, cols], preferred_element_type=F32) + bias_ref[:, cols]
            u = _gelu(su_ref[rs, cols].astype(F32))
            gate = _sigmoid(gs_ref[rs, cols].astype(F32))
            o_ref[rs, cols] = (u * mixed * gate).astype(o_ref.dtype)


def _sgu(proj2, ln_g, ln_b, w_s, b_s, d, rows=512):
    n = proj2.shape[0]
    bias = jnp.repeat(b_s.T, d // SG_GROUPS, axis=1)
    col = lambda off: (lambda r: (r, off))
    full = lambda shape: pl.BlockSpec(shape, lambda r: (0,) * len(shape))
    return pl.pallas_call(
        functools.partial(_sgu_kernel, rows=rows),
        grid=(n // rows,),
        in_specs=[pl.BlockSpec((rows, d), col(2)),
                  pl.BlockSpec((rows, d), col(3)),
                  pl.BlockSpec((rows, d), col(8)),
                  full((1, d)), full((1, d)),
                  full((SG_GROUPS, SG_CHUNK, SG_CHUNK)), full((SG_CHUNK, d))],
        out_specs=pl.BlockSpec((rows, d), lambda r: (r, 0)),
        out_shape=jax.ShapeDtypeStruct((n, d), BF16),
        compiler_params=_params("parallel"),
        name="sgu",
    )(proj2, proj2, proj2, ln_g.reshape(1, d), ln_b.reshape(1, d), w_s, bias)


def _moba_kernel(q_ref, k_ref, v_ref, gm_ref, o_ref, kaug, vaug, gate_sc, s_sc, p_sc, *, seq):
    blk = MOBA_BLOCK
    nblk = seq // blk
    c2 = (HEAD_DIM ** -0.5) * math.log2(math.e)
    contract_last = (((1,), (1,)), ((), ()))

    row = lax.broadcasted_iota(jnp.int32, (LANES, seq), 0)
    pos = lax.broadcasted_iota(jnp.int32, (LANES, seq), 1)
    ind = jnp.where((pos >= row * blk) & (pos < (row + 1) * blk), 1.0, 0.0).astype(BF16)
    kmean = (jnp.dot(ind, k_ref[...], preferred_element_type=F32) * (1.0 / blk)).astype(BF16)
    gate_sc[...] = lax.dot_general(q_ref[...], kmean, contract_last, preferred_element_type=F32)
    kpos_all = lax.broadcasted_iota(jnp.int32, (seq, LANES), 0)
    kblk_all = lax.broadcasted_iota(jnp.int32, (seq, LANES), 1) * blk
    kaug[:, 0:HEAD_DIM] = k_ref[...]
    kaug[:, HEAD_DIM:] = jnp.where((kpos_all >= kblk_all) & (kpos_all < kblk_all + blk),
                                   1.0, 0.0).astype(BF16)
    vaug[:, 0:HEAD_DIM] = v_ref[...]
    vaug[:, HEAD_DIM:] = jnp.ones((seq, HEAD_DIM), BF16)

    lane = lax.broadcasted_iota(jnp.int32, (blk, LANES), 1)
    qpos = lax.broadcasted_iota(jnp.int32, (blk, blk), 0)
    kpos = lax.broadcasted_iota(jnp.int32, (blk, blk), 1)
    n_sel = min(MOBA_TOPK, nblk)

    def halves_max(s):
        return jnp.maximum(s[:, 0:LANES], s[:, LANES:2 * LANES])

    for qi in range(nblk):
        rows = slice(qi * blk, (qi + 1) * blk)
        q = q_ref[rows, :]
        s_own = lax.dot_general(q, k_ref[rows, :], contract_last, preferred_element_type=F32)
        s_own = jnp.where(kpos <= qpos, s_own, NEG)
        macc = halves_max(s_own)
        if qi > 0:
            valid = lane < qi
            if qi > n_sel:
                g = jnp.where(valid, gate_sc[rows, :], -jnp.inf)
                sel = jnp.zeros((blk, LANES), jnp.int32)
                for _ in range(n_sel):
                    gm = jnp.max(g, axis=-1, keepdims=True)
                    idx = jnp.min(jnp.where(g == gm, lane, LANES), axis=-1, keepdims=True)
                    pick = lane == idx
                    sel = jnp.where(pick, 1, sel)
                    g = jnp.where(pick, -jnp.inf, g)
                attended = (sel > 0) & valid
            else:
                attended = valid
            q_aug = jnp.concatenate([q, jnp.where(attended, 0.0, NEG).astype(BF16)], axis=-1)
            for j in range(qi):
                cols = slice(j * blk, (j + 1) * blk)
                sj = lax.dot_general(q_aug, kaug[cols, :], contract_last,
                                     preferred_element_type=F32)
                s_sc[:, cols] = sj
                macc = jnp.maximum(macc, halves_max(sj))
        m = jnp.max(macc, axis=-1, keepdims=True)
        for j in range(qi):
            cols = slice(j * blk, (j + 1) * blk)
            p_sc[:, cols] = jnp.exp2((s_sc[:, cols] - m) * c2).astype(BF16)
        p_sc[:, rows] = jnp.exp2((s_own - m) * c2).astype(BF16)
        acc = jnp.dot(p_sc[:, 0:(qi + 1) * blk], vaug[0:(qi + 1) * blk, :],
                      preferred_element_type=F32)
        y = acc[:, 0:HEAD_DIM] / acc[:, HEAD_DIM:]
        o_ref[rows, :] = (y * _sigmoid(gm_ref[rows, :].astype(F32))).astype(o_ref.dtype)


def _moba(proj2, nb, seq, d):
    n = proj2.shape[0]
    hpd = d // HEAD_DIM
    col = lambda off: (lambda b, h: (b, off * hpd + h))
    return pl.pallas_call(
        functools.partial(_moba_kernel, seq=seq),
        grid=(nb, ATT_HEADS),
        in_specs=[pl.BlockSpec((seq, HEAD_DIM), col(4)),
                  pl.BlockSpec((seq, HEAD_DIM), col(5)),
                  pl.BlockSpec((seq, HEAD_DIM), col(6)),
                  pl.BlockSpec((seq, HEAD_DIM), col(9))],
        out_specs=pl.BlockSpec((seq, HEAD_DIM), lambda b, h: (b, h)),
        out_shape=jax.ShapeDtypeStruct((n, d), BF16),
        scratch_shapes=[pltpu.VMEM((seq, HEAD_DIM + LANES), BF16),
                        pltpu.VMEM((seq, 2 * HEAD_DIM), BF16),
                        pltpu.VMEM((seq, LANES), F32),
                        pltpu.VMEM((MOBA_BLOCK, seq), F32),
                        pltpu.VMEM((MOBA_BLOCK, seq), BF16)],
        compiler_params=_params("parallel", "parallel"),
        name="moba",
    )(proj2, proj2, proj2, proj2)


def _out_proj_kernel(ya_ref, ys_ref, ym_ref, x_ref, w_ref, g_ref, b_ref, o_ref):
    merged = (ya_ref[...].astype(F32) + ys_ref[...].astype(F32) + ym_ref[...].astype(F32))
    mix = jnp.dot(merged.astype(BF16), w_ref[...], preferred_element_type=F32)
    o_ref[...] = _layer_norm(DEEPNORM_ALPHA * x_ref[...] + mix, g_ref[...], b_ref[...])


def _out_proj(ya, ys, ym, x2, w_bf, g, b, tm=512):
    n, d = x2.shape
    rowblk = pl.BlockSpec((tm, d), lambda i: (i, 0))
    full = lambda shape: pl.BlockSpec(shape, lambda i: (0,) * len(shape))
    return pl.pallas_call(
        _out_proj_kernel,
        grid=(n // tm,),
        in_specs=[rowblk, rowblk, rowblk, rowblk, full((d, d)), full((1, d)), full((1, d))],
        out_specs=rowblk,
        out_shape=jax.ShapeDtypeStruct((n, d), F32),
        compiler_params=_params("parallel"),
        name="out_proj",
    )(ya, ys, ym, x2, w_bf, g.reshape(1, d), b.reshape(1, d))


def _ffn_kernel(x_ref, xh_ref, wup_ref, cw_ref, cb_ref, wdn_ref, g_ref, b_ref, o_ref,
                *, tm, tf, d_ff, seq):
    i = pl.program_id(0)
    halo = SUBLANES
    x = x_ref[...]
    xb = x.astype(BF16)
    seq_start = (i * tm) % seq == 0
    xh = jnp.where(seq_start, 0.0, xh_ref[...]).astype(BF16)
    acc = jnp.zeros(x.shape, F32)
    for c in range(d_ff // tf):
        cols = slice(c * tf, (c + 1) * tf)
        wg = wup_ref[:, cols]
        hg = jnp.dot(xb, wg, preferred_element_type=F32)
        hh = jnp.dot(xh, wg, preferred_element_type=F32)
        hu = jnp.dot(xb, wup_ref[:, d_ff + c * tf:d_ff + (c + 1) * tf],
                     preferred_element_type=F32)
        ext = jnp.concatenate([hh, hg], axis=0)
        cw = cw_ref[:, cols]
        gate = cw[2:3] * hg + cb_ref[:, cols]
        for j in range(1, FFN_CONV):
            gate = gate + cw[2 - j:3 - j] * ext[halo - j:halo - j + tm, :]
        act = (_gelu(gate) * hu).astype(BF16)
        acc = acc + jnp.dot(act, wdn_ref[cols, :], preferred_element_type=F32)
    o_ref[...] = _layer_norm(DEEPNORM_ALPHA * x + acc, g_ref[...], b_ref[...])


def _ffn(x2, w_up_bf, conv_w, conv_b, w_dn_bf, g, b, seq, tm=512, tf=512):
    n, d = x2.shape
    d_ff = w_dn_bf.shape[0]
    hb = tm // SUBLANES
    full = lambda shape: pl.BlockSpec(shape, lambda i: (0,) * len(shape),
                                      pipeline_mode=pl.Buffered(1))
    return pl.pallas_call(
        functools.partial(_ffn_kernel, tm=tm, tf=tf, d_ff=d_ff, seq=seq),
        grid=(n // tm,),
        in_specs=[pl.BlockSpec((tm, d), lambda i: (i, 0)),
                  pl.BlockSpec((SUBLANES, d), lambda i: (jnp.maximum(i * hb - 1, 0), 0)),
                  full((d, 2 * d_ff)), full((FFN_CONV, d_ff)), full((1, d_ff)),
                  full((d_ff, d)), full((1, d)), full((1, d))],
        out_specs=pl.BlockSpec((tm, d), lambda i: (i, 0)),
        out_shape=jax.ShapeDtypeStruct((n, d), F32),
        compiler_params=_params("parallel"),
        name="ffn",
    )(x2, x2, w_up_bf, conv_w, conv_b.reshape(1, d_ff), w_dn_bf, g.reshape(1, d), b.reshape(1, d))


def kernel(x, w_in, conv_rg_w, conv_rg_b, w_rgate, b_rgate, w_igate, b_igate, lru_lambda, sgu_ln_g, sgu_ln_b, w_spatial, b_spatial, w_out, ln_mix_g, ln_mix_b, w_ffn_up, conv_ffn_w, conv_ffn_b, w_ffn_down, ln_ffn_g, ln_ffn_b):
    nb, seq, d = x.shape
    n = nb * seq
    depth = w_in.shape[0]
    x2 = x.reshape(n, d)
    for l in range(depth):
        proj = _in_proj(x2, w_in[l].astype(BF16))
        ya = _rglru(proj.reshape(nb, seq, -1), conv_rg_w[l], conv_rg_b[l], w_rgate[l], b_rgate[l],
                    w_igate[l], b_igate[l], lru_lambda[l], d).reshape(n, d)
        ys = _sgu(proj, sgu_ln_g[l], sgu_ln_b[l], w_spatial[l], b_spatial[l], d)
        ym = _moba(proj, nb, seq, d)
        x2 = _out_proj(ya, ys, ym, x2, w_out[l].astype(BF16), ln_mix_g[l], ln_mix_b[l])
        x2 = _ffn(x2, w_ffn_up[l].astype(BF16), conv_ffn_w[l], conv_ffn_b[l],
                  w_ffn_down[l].astype(BF16), ln_ffn_g[l], ln_ffn_b[l], seq)
    return x2.reshape(nb, seq, d)
```

```python
import functools
import math

import jax
import jax.numpy as jnp
from jax import lax
from jax.experimental import pallas as pl
from jax.experimental.pallas import tpu as pltpu

F32 = jnp.float32
BF16 = jnp.bfloat16

RG_BLOCKS = 8
RG_CONV = 4
RG_C = 8.0
SG_GROUPS = 8
SG_CHUNK = 128
ATT_HEADS = 8
HEAD_DIM = 128
MOBA_BLOCK = 256
MOBA_TOPK = 3
FFN_CONV = 3
LN_EPS = 1e-5
DEPTH = 2
DEEPNORM_ALPHA = (2 * DEPTH) ** 0.25
NEG = -1e30

LANES = 128
SUBLANES = 8
VMEM_LIMIT = 56 * 1024 * 1024

_SQRT_2_OVER_PI = math.sqrt(2.0 / math.pi)


def _gelu(x):
    return x * (0.5 * (1.0 + jnp.tanh(_SQRT_2_OVER_PI * (x + 0.044715 * (x * x * x)))))


def _sigmoid(x):
    return 0.5 * jnp.tanh(0.5 * x) + 0.5


def _sqrt_nonneg(y):
    return jnp.where(y > 0.0, y * lax.rsqrt(y), 0.0)


def _layer_norm(y, g, b):
    mu = jnp.mean(y, axis=-1, keepdims=True)
    d = y - mu
    var = jnp.mean(d * d, axis=-1, keepdims=True)
    return d * lax.rsqrt(var + LN_EPS) * g + b


def _params(*sem):
    return pltpu.CompilerParams(dimension_semantics=sem, vmem_limit_bytes=VMEM_LIMIT)


def _in_proj_kernel(x_ref, w_ref, o_ref):
    o_ref[...] = jnp.dot(x_ref[...].astype(BF16), w_ref[...],
                         preferred_element_type=F32).astype(o_ref.dtype)


def _in_proj(x2, w_bf, tm=1024, tn=2048):
    n, d = x2.shape
    d_in = w_bf.shape[1]
    return pl.pallas_call(
        _in_proj_kernel,
        grid=(n // tm, d_in // tn),
        in_specs=[pl.BlockSpec((tm, d), lambda i, j: (i, 0)),
                  pl.BlockSpec((d, tn), lambda i, j: (0, j))],
        out_specs=pl.BlockSpec((tm, tn), lambda i, j: (i, j)),
        out_shape=jax.ShapeDtypeStruct((n, d_in), BF16),
        compiler_params=_params("parallel", "arbitrary"),
        name="in_proj",
    )(x2, w_bf)


def _rglru_kernel(ax_ref, ag_ref, ga_ref, cw_ref, cb_ref, wr_ref, br_ref, wi_ref, bi_ref,
                  lam_ref, o_ref, xbuf, a_sc, u_sc, hcarry, *, nb, t, cb, sp):
    i = pl.program_id(1)
    halo = SUBLANES

    @pl.when(i == 0)
    def _():
        xbuf[:, 0:halo, :] = jnp.zeros((nb, halo, cb), F32)
        hcarry[...] = jnp.zeros_like(hcarry)

    lam = lam_ref[...]
    neg_lam = -lam
    softplus = jnp.maximum(neg_lam, 0.0) + jnp.log1p(jnp.exp(-jnp.abs(neg_lam)))
    decay_rate = -RG_C * softplus
    cw = cw_ref[...]
    cbias = cb_ref[...]
    nsub = cb // LANES

    def gates(b, carry):
        x = ax_ref[b].astype(F32)
        xbuf[b, halo:halo + t, :] = x
        xc = cw[3:4] * x + cbias
        for j in range(1, RG_CONV):
            xc = xc + cw[3 - j:4 - j] * xbuf[b, halo - j:halo - j + t, :]
        xbuf[b, 0:halo, :] = xbuf[b, t:t + halo, :]
        xcb = xc.astype(BF16)
        r_parts, i_parts = [], []
        for s in range(nsub):
            xs = xcb[:, s * LANES:(s + 1) * LANES]
            r_parts.append(jnp.dot(xs, wr_ref[s].astype(BF16), preferred_element_type=F32))
            i_parts.append(jnp.dot(xs, wi_ref[s].astype(BF16), preferred_element_type=F32))
        r = _sigmoid(jnp.concatenate(r_parts, axis=-1) + br_ref[...])
        ig = _sigmoid(jnp.concatenate(i_parts, axis=-1) + bi_ref[...])
        log_a = decay_rate * r
        a = jnp.exp(log_a)
        u = _sqrt_nonneg(1.0 - a * a) * (ig * xc)
        row0 = pl.multiple_of(b * sp, SUBLANES)
        for s in range(nsub):
            a_sc[s, pl.ds(row0, t), :] = a[:, s * LANES:(s + 1) * LANES]
            u_sc[s, pl.ds(row0, t), :] = u[:, s * LANES:(s + 1) * LANES]
        return carry

    lax.fori_loop(0, nb, gates, 0)

    def step(p, hs):
        new = []
        for s in range(nsub):
            rows = pl.ds(p, nb, stride=sp)
            h = a_sc[s, rows, :] * hs[s] + u_sc[s, rows, :]
            u_sc[s, rows, :] = h
            new.append(h)
        return tuple(new)

    hs = lax.fori_loop(0, t, step, tuple(hcarry[s] for s in range(nsub)), unroll=8)
    for s in range(nsub):
        hcarry[s] = hs[s]

    def emit(b, carry):
        row0 = pl.multiple_of(b * sp, SUBLANES)
        h = jnp.concatenate([u_sc[s, pl.ds(row0, t), :] for s in range(nsub)], axis=-1)
        y = h * _gelu(ag_ref[b].astype(F32)) * _sigmoid(ga_ref[b].astype(F32))
        o_ref[b] = y.astype(o_ref.dtype)
        return carry

    lax.fori_loop(0, nb, emit, 0)


def _rglru(proj3, conv_w, conv_b, w_r, b_r, w_i, b_i, lam, d, t=256, cb=512):
    nb, s, _ = proj3.shape
    ncb = d // cb
    nsub = cb // LANES
    sp = t + SUBLANES
    row = lambda a: a.reshape(1, d)
    kern = functools.partial(_rglru_kernel, nb=nb, t=t, cb=cb, sp=sp)
    col = lambda off: (lambda c, i: (0, i, off * ncb + c))
    vec = pl.BlockSpec((1, cb), lambda c, i: (0, c))
    wspec = pl.BlockSpec((nsub, LANES, LANES), lambda c, i: (c, 0, 0))
    return pl.pallas_call(
        kern,
        grid=(ncb, s // t),
        in_specs=[pl.BlockSpec((nb, t, cb), col(0)),
                  pl.BlockSpec((nb, t, cb), col(1)),
                  pl.BlockSpec((nb, t, cb), col(7)),
                  pl.BlockSpec((RG_CONV, cb), lambda c, i: (0, c)),
                  vec, wspec, vec, wspec, vec, vec],
        out_specs=pl.BlockSpec((nb, t, cb), lambda c, i: (0, i, c)),
        out_shape=jax.ShapeDtypeStruct((nb, s, d), BF16),
        scratch_shapes=[pltpu.VMEM((nb, t + SUBLANES, cb), F32),
                        pltpu.VMEM((nsub, nb * sp, LANES), F32),
                        pltpu.VMEM((nsub, nb * sp, LANES), F32),
                        pltpu.VMEM((nsub, nb, LANES), F32)],
        compiler_params=_params("parallel", "arbitrary"),
        name="rglru",
    )(proj3, proj3, proj3, conv_w, row(conv_b), w_r, row(b_r), w_i, row(b_i), row(lam))


def _sgu_kernel(su_ref, sv_ref, gs_ref, lg_ref, lb_ref, ws_ref, bias_ref, o_ref, *, rows):
    v = _layer_norm(_gelu(sv_ref[...].astype(F32)), lg_ref[...], lb_ref[...]).astype(BF16)
    tri = (lax.broadcasted_iota(jnp.int32, (SG_CHUNK, SG_CHUNK), 1)
           <= lax.broadcasted_iota(jnp.int32, (SG_CHUNK, SG_CHUNK), 0))
    for g in range(SG_GROUPS):
        w = jnp.where(tri, ws_ref[g], 0.0).astype(BF16)
        cols = slice(g * LANES, (g + 1) * LANES)
        for c in range(rows // SG_CHUNK):
            rs = slice(c * SG_CHUNK, (c + 1) * SG_CHUNK)
            mixed = jnp.dot(w, v[rs, cols], preferred_element_type=F32) + bias_ref[:, cols]
            u = _gelu(su_ref[rs, cols].astype(F32))
            gate = _sigmoid(gs_ref[rs, cols].astype(F32))
            o_ref[rs, cols] = (u * mixed * gate).astype(o_ref.dtype)


def _sgu(proj2, ln_g, ln_b, w_s, b_s, d, rows=512):
    n = proj2.shape[0]
    bias = jnp.repeat(b_s.T, d // SG_GROUPS, axis=1)
    col = lambda off: (lambda r: (r, off))
    full = lambda shape: pl.BlockSpec(shape, lambda r: (0,) * len(shape))
    return pl.pallas_call(
        functools.partial(_sgu_kernel, rows=rows),
        grid=(n // rows,),
        in_specs=[pl.BlockSpec((rows, d), col(2)),
                  pl.BlockSpec((rows, d), col(3)),
                  pl.BlockSpec((rows, d), col(8)),
                  full((1, d)), full((1, d)),
                  full((SG_GROUPS, SG_CHUNK, SG_CHUNK)), full((SG_CHUNK, d))],
        out_specs=pl.BlockSpec((rows, d), lambda r: (r, 0)),
        out_shape=jax.ShapeDtypeStruct((n, d), BF16),
        compiler_params=_params("parallel"),
        name="sgu",
    )(proj2, proj2, proj2, ln_g.reshape(1, d), ln_b.reshape(1, d), w_s, bias)


def _moba_kernel(q_ref, k_ref, v_ref, gm_ref, o_ref, kaug, vaug, gate_sc, s_sc, p_sc, mx_sc, *, seq):
    blk = MOBA_BLOCK
    nblk = seq // blk
    c2 = (HEAD_DIM ** -0.5) * math.log2(math.e)
    contract_last = (((1,), (1,)), ((), ()))

    row = lax.broadcasted_iota(jnp.int32, (LANES, seq), 0)
    pos = lax.broadcasted_iota(jnp.int32, (LANES, seq), 1)
    ind = jnp.where((pos >= row * blk) & (pos < (row + 1) * blk), 1.0, 0.0).astype(BF16)
    kmean = (jnp.dot(ind, k_ref[...], preferred_element_type=F32) * (1.0 / blk)).astype(BF16)
    gate_sc[...] = lax.dot_general(q_ref[...], kmean, contract_last, preferred_element_type=F32)
    kpos_all = lax.broadcasted_iota(jnp.int32, (seq, LANES), 0)
    kblk_all = lax.broadcasted_iota(jnp.int32, (seq, LANES), 1) * blk
    kaug[:, 0:HEAD_DIM] = k_ref[...]
    kaug[:, HEAD_DIM:] = jnp.where((kpos_all >= kblk_all) & (kpos_all < kblk_all + blk),
                                   1.0, 0.0).astype(BF16)
    vaug[:, 0:HEAD_DIM] = v_ref[...]
    vaug[:, HEAD_DIM:] = jnp.ones((seq, HEAD_DIM), BF16)

    lane = lax.broadcasted_iota(jnp.int32, (blk, LANES), 1)
    qpos = lax.broadcasted_iota(jnp.int32, (blk, blk), 0)
    kpos = lax.broadcasted_iota(jnp.int32, (blk, blk), 1)
    n_sel = min(MOBA_TOPK, nblk)

    def halves_max(s):
        return jnp.maximum(s[:, 0:LANES], s[:, LANES:2 * LANES])

    def scores(qi):
        buf = qi % 2
        rows = slice(qi * blk, (qi + 1) * blk)
        q = q_ref[rows, :]
        s_own = lax.dot_general(q, k_ref[rows, :], contract_last, preferred_element_type=F32)
        s_own = jnp.where(kpos <= qpos, s_own, NEG)
        s_sc[buf, :, rows] = s_own
        macc = halves_max(s_own)
        if qi > 0:
            valid = lane < qi
            if qi > n_sel:
                g = jnp.where(valid, gate_sc[rows, :], -jnp.inf)
                sel = jnp.zeros((blk, LANES), jnp.int32)
                for _ in range(n_sel):
                    gm = jnp.max(g, axis=-1, keepdims=True)
                    idx = jnp.min(jnp.where(g == gm, lane, LANES), axis=-1, keepdims=True)
                    pick = lane == idx
                    sel = jnp.where(pick, 1, sel)
                    g = jnp.where(pick, -jnp.inf, g)
                attended = (sel > 0) & valid
            else:
                attended = valid
            q_aug = jnp.concatenate([q, jnp.where(attended, 0.0, NEG).astype(BF16)], axis=-1)
            for j in range(qi):
                cols = slice(j * blk, (j + 1) * blk)
                sj = lax.dot_general(q_aug, kaug[cols, :], contract_last,
                                     preferred_element_type=F32)
                s_sc[buf, :, cols] = sj
                macc = jnp.maximum(macc, halves_max(sj))
        mx_sc[buf] = macc

    def outputs(qi):
        buf = qi % 2
        rows = slice(qi * blk, (qi + 1) * blk)
        m = jnp.max(mx_sc[buf], axis=-1, keepdims=True)
        for j in range(qi + 1):
            cols = slice(j * blk, (j + 1) * blk)
            p_sc[buf, :, cols] = jnp.exp2((s_sc[buf, :, cols] - m) * c2).astype(BF16)
        acc = jnp.dot(p_sc[buf, :, 0:(qi + 1) * blk], vaug[0:(qi + 1) * blk, :],
                      preferred_element_type=F32)
        y = acc[:, 0:HEAD_DIM] / acc[:, HEAD_DIM:]
        o_ref[rows, :] = (y * _sigmoid(gm_ref[rows, :].astype(F32))).astype(o_ref.dtype)

    scores(0)
    for qi in range(nblk):
        if qi + 1 < nblk:
            scores(qi + 1)
        outputs(qi)


def _moba(proj2, nb, seq, d):
    n = proj2.shape[0]
    hpd = d // HEAD_DIM
    col = lambda off: (lambda b, h: (b, off * hpd + h))
    return pl.pallas_call(
        functools.partial(_moba_kernel, seq=seq),
        grid=(nb, ATT_HEADS),
        in_specs=[pl.BlockSpec((seq, HEAD_DIM), col(4)),
                  pl.BlockSpec((seq, HEAD_DIM), col(5)),
                  pl.BlockSpec((seq, HEAD_DIM), col(6)),
                  pl.BlockSpec((seq, HEAD_DIM), col(9))],
        out_specs=pl.BlockSpec((seq, HEAD_DIM), lambda b, h: (b, h)),
        out_shape=jax.ShapeDtypeStruct((n, d), BF16),
        scratch_shapes=[pltpu.VMEM((seq, HEAD_DIM + LANES), BF16),
                        pltpu.VMEM((seq, 2 * HEAD_DIM), BF16),
                        pltpu.VMEM((seq, LANES), F32),
                        pltpu.VMEM((2, MOBA_BLOCK, seq), F32),
                        pltpu.VMEM((2, MOBA_BLOCK, seq), BF16),
                        pltpu.VMEM((2, MOBA_BLOCK, LANES), F32)],
        compiler_params=_params("parallel", "parallel"),
        name="moba",
    )(proj2, proj2, proj2, proj2)


def _out_proj_kernel(ya_ref, ys_ref, ym_ref, x_ref, w_ref, g_ref, b_ref, o_ref):
    merged = (ya_ref[...].astype(F32) + ys_ref[...].astype(F32) + ym_ref[...].astype(F32))
    mix = jnp.dot(merged.astype(BF16), w_ref[...], preferred_element_type=F32)
    o_ref[...] = _layer_norm(DEEPNORM_ALPHA * x_ref[...] + mix, g_ref[...], b_ref[...])


def _out_proj(ya, ys, ym, x2, w_bf, g, b, tm=512):
    n, d = x2.shape
    rowblk = pl.BlockSpec((tm, d), lambda i: (i, 0))
    full = lambda shape: pl.BlockSpec(shape, lambda i: (0,) * len(shape))
    return pl.pallas_call(
        _out_proj_kernel,
        grid=(n // tm,),
        in_specs=[rowblk, rowblk, rowblk, rowblk, full((d, d)), full((1, d)), full((1, d))],
        out_specs=rowblk,
        out_shape=jax.ShapeDtypeStruct((n, d), F32),
        compiler_params=_params("parallel"),
        name="out_proj",
    )(ya, ys, ym, x2, w_bf, g.reshape(1, d), b.reshape(1, d))


def _ffn_kernel(x_ref, xh_ref, wup_ref, cw_ref, cb_ref, wdn_ref, g_ref, b_ref, o_ref, act_sc,
                *, tm, tf, d_ff, seq):
    i = pl.program_id(0)
    halo = SUBLANES
    x = x_ref[...]
    xb = x.astype(BF16)
    seq_start = (i * tm) % seq == 0
    xh = jnp.where(seq_start, 0.0, xh_ref[...]).astype(BF16)
    for c in range(d_ff // tf):
        cols = slice(c * tf, (c + 1) * tf)
        wg = wup_ref[:, cols]
        hg = jnp.dot(xb, wg, preferred_element_type=F32)
        hh = jnp.dot(xh, wg, preferred_element_type=F32)
        hu = jnp.dot(xb, wup_ref[:, d_ff + c * tf:d_ff + (c + 1) * tf],
                     preferred_element_type=F32)
        ext = jnp.concatenate([hh, hg], axis=0)
        cw = cw_ref[:, cols]
        gate = cw[2:3] * hg + cb_ref[:, cols]
        for j in range(1, FFN_CONV):
            gate = gate + cw[2 - j:3 - j] * ext[halo - j:halo - j + tm, :]
        act_sc[:, cols] = (_gelu(gate) * hu).astype(BF16)
    acc = jnp.dot(act_sc[...], wdn_ref[...], preferred_element_type=F32)
    o_ref[...] = _layer_norm(DEEPNORM_ALPHA * x + acc, g_ref[...], b_ref[...])


def _ffn(x2, w_up_bf, conv_w, conv_b, w_dn_bf, g, b, seq, tm=1024, tf=256):
    n, d = x2.shape
    d_ff = w_dn_bf.shape[0]
    hb = tm // SUBLANES
    full = lambda shape: pl.BlockSpec(shape, lambda i: (0,) * len(shape),
                                      pipeline_mode=pl.Buffered(1))
    return pl.pallas_call(
        functools.partial(_ffn_kernel, tm=tm, tf=tf, d_ff=d_ff, seq=seq),
        grid=(n // tm,),
        in_specs=[pl.BlockSpec((tm, d), lambda i: (i, 0)),
                  pl.BlockSpec((SUBLANES, d), lambda i: (jnp.maximum(i * hb - 1, 0), 0)),
                  full((d, 2 * d_ff)), full((FFN_CONV, d_ff)), full((1, d_ff)),
                  full((d_ff, d)), full((1, d)), full((1, d))],
        out_specs=pl.BlockSpec((tm, d), lambda i: (i, 0)),
        out_shape=jax.ShapeDtypeStruct((n, d), F32),
        scratch_shapes=[pltpu.VMEM((tm, d_ff), BF16)],
        compiler_params=_params("parallel"),
        name="ffn",
    )(x2, x2, w_up_bf, conv_w, conv_b.reshape(1, d_ff), w_dn_bf, g.reshape(1, d), b.reshape(1, d))


def kernel(x, w_in, conv_rg_w, conv_rg_b, w_rgate, b_rgate, w_igate, b_igate, lru_lambda, sgu_ln_g, sgu_ln_b, w_spatial, b_spatial, w_out, ln_mix_g, ln_mix_b, w_ffn_up, conv_ffn_w, conv_ffn_b, w_ffn_down, ln_ffn_g, ln_ffn_b):
    nb, seq, d = x.shape
    n = nb * seq
    depth = w_in.shape[0]
    x2 = x.reshape(n, d)
    for l in range(depth):
        proj = _in_proj(x2, w_in[l].astype(BF16))
        ya = _rglru(proj.reshape(nb, seq, -1), conv_rg_w[l], conv_rg_b[l], w_rgate[l], b_rgate[l],
                    w_igate[l], b_igate[l], lru_lambda[l], d).reshape(n, d)
        ys = _sgu(proj, sgu_ln_g[l], sgu_ln_b[l], w_spatial[l], b_spatial[l], d)
        ym = _moba(proj, nb, seq, d)
        x2 = _out_proj(ya, ys, ym, x2, w_out[l].astype(BF16), ln_mix_g[l], ln_mix_b[l])
        x2 = _ffn(x2, w_ffn_up[l].astype(BF16), conv_ffn_w[l], conv_ffn_b[l],
                  w_ffn_down[l].astype(BF16), ln_ffn_g[l], ln_ffn_b[l], seq)
    return x2.reshape(nb, seq, d)
```

```python
import functools
import math

import jax
import jax.numpy as jnp
from jax import lax
from jax.experimental import pallas as pl
from jax.experimental.pallas import tpu as pltpu

F32 = jnp.float32
BF16 = jnp.bfloat16

RG_CONV = 4
RG_C = 8.0
SG_GROUPS = 8
SG_CHUNK = 128
ATT_HEADS = 8
HEAD_DIM = 128
MOBA_BLOCK = 256
MOBA_TOPK = 3
FFN_CONV = 3
LN_EPS = 1e-5
DEPTH = 2
DEEPNORM_ALPHA = (2 * DEPTH) ** 0.25
NEG = -1e30

LANES = 128
SUBLANES = 8
VMEM_LIMIT = 56 * 1024 * 1024

_SQRT_2_OVER_PI = math.sqrt(2.0 / math.pi)


def _gelu(x):
    return x * (0.5 * (1.0 + jnp.tanh(_SQRT_2_OVER_PI * (x + 0.044715 * (x * x * x)))))


def _sigmoid(x):
    return 0.5 * jnp.tanh(0.5 * x) + 0.5


def _sqrt_nonneg(y):
    return jnp.where(y > 0.0, y * lax.rsqrt(y), 0.0)


def _layer_norm(y, g, b):
    mu = jnp.mean(y, axis=-1, keepdims=True)
    d = y - mu
    var = jnp.mean(d * d, axis=-1, keepdims=True)
    return d * lax.rsqrt(var + LN_EPS) * g + b


def _params(*sem):
    return pltpu.CompilerParams(dimension_semantics=sem, vmem_limit_bytes=VMEM_LIMIT)


def _in_proj_kernel(x_ref, w_ref, o_ref):
    o_ref[...] = jnp.dot(x_ref[...].astype(BF16), w_ref[...],
                         preferred_element_type=F32).astype(o_ref.dtype)


def _in_proj(x2, w_bf, tm=1024, tn=2048):
    n, d = x2.shape
    d_in = w_bf.shape[1]
    return pl.pallas_call(
        _in_proj_kernel,
        grid=(n // tm, d_in // tn),
        in_specs=[pl.BlockSpec((tm, d), lambda i, j: (i, 0)),
                  pl.BlockSpec((d, tn), lambda i, j: (0, j))],
        out_specs=pl.BlockSpec((tm, tn), lambda i, j: (i, j)),
        out_shape=jax.ShapeDtypeStruct((n, d_in), BF16),
        compiler_params=_params("parallel", "arbitrary"),
        name="in_proj",
    )(x2, w_bf)


def _rglru_kernel(x_ref, wx_ref, wg_ref, wm_ref, cw_ref, cb_ref, wri_ref, br_ref, bi_ref,
                  lam_ref, o_ref, xbuf, xc_sc, g_sc, a_sc, u_sc, hcarry, *, nb, t, cbw, sp):
    i = pl.program_id(0)
    halo = SUBLANES
    d = x_ref.shape[-1]
    nblk = d // LANES

    @pl.when(i == 0)
    def _():
        xbuf[:, 0:halo, :] = jnp.zeros((nb, halo, d), F32)
        hcarry[...] = jnp.zeros_like(hcarry)

    xb = x_ref[...].reshape(nb * t, d).astype(BF16)
    neg_lam = -lam_ref[...]
    softplus = jnp.maximum(neg_lam, 0.0) + jnp.log1p(jnp.exp(-jnp.abs(neg_lam)))
    decay_rate = -RG_C * softplus

    for c in range(d // cbw):
        cols = slice(c * cbw, (c + 1) * cbw)
        ax = jnp.dot(xb, wx_ref[:, cols], preferred_element_type=F32)
        cw = cw_ref[:, cols]
        for b in range(nb):
            x = ax[b * t:(b + 1) * t, :]
            xbuf[b, halo:halo + t, cols] = x
            xc = cw[3:4] * x + cb_ref[:, cols]
            for j in range(1, RG_CONV):
                xc = xc + cw[3 - j:4 - j] * xbuf[b, halo - j:halo - j + t, cols]
            xbuf[b, 0:halo, cols] = xbuf[b, t:t + halo, cols]
            xc_sc[b * t:(b + 1) * t, cols] = xc
        g_sc[:, cols] = (_gelu(jnp.dot(xb, wg_ref[:, cols], preferred_element_type=F32))
                         * _sigmoid(jnp.dot(xb, wm_ref[:, cols], preferred_element_type=F32))
                         ).astype(BF16)

    for k in range(nblk):
        lanes = slice(k * LANES, (k + 1) * LANES)
        xc = xc_sc[:, lanes]
        logits = jnp.dot(xc.astype(BF16), wri_ref[k].astype(BF16),
                         preferred_element_type=F32)
        r = _sigmoid(logits[:, 0:LANES] + br_ref[:, lanes])
        ig = _sigmoid(logits[:, LANES:] + bi_ref[:, lanes])
        a = jnp.exp(decay_rate[:, lanes] * r)
        u = _sqrt_nonneg(1.0 - a * a) * (ig * xc)
        for b in range(nb):
            a_sc[k, b * sp:b * sp + t, :] = a[b * t:(b + 1) * t, :]
            u_sc[k, b * sp:b * sp + t, :] = u[b * t:(b + 1) * t, :]

    hs = [hcarry[k] for k in range(nblk)]
    for p in range(t):
        rows = pl.ds(p, nb, stride=sp)
        for k in range(nblk):
            hs[k] = a_sc[k, rows, :] * hs[k] + u_sc[k, rows, :]
            u_sc[k, rows, :] = hs[k]
    for k in range(nblk):
        hcarry[k] = hs[k]

    for b in range(nb):
        h = jnp.concatenate([u_sc[k, b * sp:b * sp + t, :] for k in range(nblk)], axis=-1)
        o_ref[b] = (h * g_sc[b * t:(b + 1) * t, :].astype(F32)).astype(o_ref.dtype)


def _rglru(x3, wx, wg, wm, conv_w, conv_b, w_r, b_r, w_i, b_i, lam, t=128, cbw=256):
    nb, s, d = x3.shape
    sp = t + SUBLANES
    row = lambda a: a.reshape(1, d)
    w_ri = jnp.concatenate([w_r, w_i], axis=-1)
    full = lambda shape: pl.BlockSpec(shape, lambda i: (0,) * len(shape),
                                      pipeline_mode=pl.Buffered(1))
    return pl.pallas_call(
        functools.partial(_rglru_kernel, nb=nb, t=t, cbw=cbw, sp=sp),
        grid=(s // t,),
        in_specs=[pl.BlockSpec((nb, t, d), lambda i: (0, i, 0)),
                  full((d, d)), full((d, d)), full((d, d)),
                  full((RG_CONV, d)), full((1, d)),
                  full(w_ri.shape), full((1, d)), full((1, d)), full((1, d))],
        out_specs=pl.BlockSpec((nb, t, d), lambda i: (0, i, 0)),
        out_shape=jax.ShapeDtypeStruct((nb, s, d), BF16),
        scratch_shapes=[pltpu.VMEM((nb, t + SUBLANES, d), F32),
                        pltpu.VMEM((nb * t, d), F32),
                        pltpu.VMEM((nb * t, d), BF16),
                        pltpu.VMEM((d // LANES, nb * sp, LANES), F32),
                        pltpu.VMEM((d // LANES, nb * sp, LANES), F32),
                        pltpu.VMEM((d // LANES, nb, LANES), F32)],
        compiler_params=_params("arbitrary"),
        name="rglru",
    )(x3, wx, wg, wm, conv_w, row(conv_b), w_ri, row(b_r), row(b_i), row(lam))


def _sgu_kernel(x_ref, wu_ref, wv_ref, wg_ref, lg_ref, lb_ref, ws_ref, bias_ref, o_ref, v_sc,
                *, rows):
    xb = x_ref[...].astype(BF16)
    sv = jnp.dot(xb, wv_ref[...], preferred_element_type=F32)
    v_sc[...] = _layer_norm(_gelu(sv), lg_ref[...], lb_ref[...]).astype(BF16)
    tri = (lax.broadcasted_iota(jnp.int32, (SG_CHUNK, SG_CHUNK), 1)
           <= lax.broadcasted_iota(jnp.int32, (SG_CHUNK, SG_CHUNK), 0))
    pair = 2 * LANES
    for gp in range(SG_GROUPS // 2):
        pcols = slice(gp * pair, (gp + 1) * pair)
        u = (_gelu(jnp.dot(xb, wu_ref[:, pcols], preferred_element_type=F32))
             * _sigmoid(jnp.dot(xb, wg_ref[:, pcols], preferred_element_type=F32)))
        for gl in range(2):
            g = 2 * gp + gl
            w = jnp.where(tri, ws_ref[g], 0.0).astype(BF16)
            cols = slice(g * LANES, (g + 1) * LANES)
            for c in range(rows // SG_CHUNK):
                rs = slice(c * SG_CHUNK, (c + 1) * SG_CHUNK)
                mixed = (jnp.dot(w, v_sc[rs, cols], preferred_element_type=F32)
                         + bias_ref[:, cols])
                o_ref[rs, cols] = (u[rs, gl * LANES:(gl + 1) * LANES] * mixed).astype(o_ref.dtype)


def _sgu(x2, wu, wv, wg, ln_g, ln_b, w_s, b_s, rows=1024):
    n, d = x2.shape
    bias = jnp.repeat(b_s.T, d // SG_GROUPS, axis=1)
    full = lambda shape: pl.BlockSpec(shape, lambda r: (0,) * len(shape),
                                      pipeline_mode=pl.Buffered(1))
    return pl.pallas_call(
        functools.partial(_sgu_kernel, rows=rows),
        grid=(n // rows,),
        in_specs=[pl.BlockSpec((rows, d), lambda r: (r, 0)),
                  full((d, d)), full((d, d)), full((d, d)),
                  full((1, d)), full((1, d)),
                  full((SG_GROUPS, SG_CHUNK, SG_CHUNK)), full((SG_CHUNK, d))],
        out_specs=pl.BlockSpec((rows, d), lambda r: (r, 0)),
        out_shape=jax.ShapeDtypeStruct((n, d), BF16),
        scratch_shapes=[pltpu.VMEM((rows, d), BF16)],
        compiler_params=_params("parallel"),
        name="sgu",
    )(x2, wu, wv, wg, ln_g.reshape(1, d), ln_b.reshape(1, d), w_s, bias)


def _moba_kernel(q_ref, k_ref, v_ref, gm_ref, o_ref, kaug, vaug, gate_sc, s_sc, p_sc, mx_sc, *, seq):
    blk = MOBA_BLOCK
    nblk = seq // blk
    c2 = (HEAD_DIM ** -0.5) * math.log2(math.e)
    contract_last = (((1,), (1,)), ((), ()))

    row = lax.broadcasted_iota(jnp.int32, (LANES, seq), 0)
    pos = lax.broadcasted_iota(jnp.int32, (LANES, seq), 1)
    ind = jnp.where((pos >= row * blk) & (pos < (row + 1) * blk), 1.0, 0.0).astype(BF16)
    kmean = (jnp.dot(ind, k_ref[...], preferred_element_type=F32) * (1.0 / blk)).astype(BF16)
    gate_sc[...] = lax.dot_general(q_ref[...], kmean, contract_last, preferred_element_type=F32)
    kpos_all = lax.broadcasted_iota(jnp.int32, (seq, LANES), 0)
    kblk_all = lax.broadcasted_iota(jnp.int32, (seq, LANES), 1) * blk
    kaug[:, 0:HEAD_DIM] = k_ref[...]
    kaug[:, HEAD_DIM:] = jnp.where((kpos_all >= kblk_all) & (kpos_all < kblk_all + blk),
                                   1.0, 0.0).astype(BF16)
    vaug[:, 0:HEAD_DIM] = v_ref[...]
    vaug[:, HEAD_DIM:] = jnp.ones((seq, HEAD_DIM), BF16)

    lane = lax.broadcasted_iota(jnp.int32, (blk, LANES), 1)
    qpos = lax.broadcasted_iota(jnp.int32, (blk, blk), 0)
    kpos = lax.broadcasted_iota(jnp.int32, (blk, blk), 1)
    n_sel = min(MOBA_TOPK, nblk)

    def halves_max(s):
        return jnp.maximum(s[:, 0:LANES], s[:, LANES:2 * LANES])

    def scores(qi):
        buf = qi % 2
        rows = slice(qi * blk, (qi + 1) * blk)
        q = q_ref[rows, :]
        s_own = lax.dot_general(q, k_ref[rows, :], contract_last, preferred_element_type=F32)
        s_own = jnp.where(kpos <= qpos, s_own, NEG)
        s_sc[buf, :, rows] = s_own
        macc = halves_max(s_own)
        if qi > 0:
            valid = lane < qi
            if qi > n_sel:
                g = jnp.where(valid, gate_sc[rows, :], -jnp.inf)
                sel = jnp.zeros((blk, LANES), jnp.int32)
                for _ in range(n_sel):
                    gm = jnp.max(g, axis=-1, keepdims=True)
                    idx = jnp.min(jnp.where(g == gm, lane, LANES), axis=-1, keepdims=True)
                    pick = lane == idx
                    sel = jnp.where(pick, 1, sel)
                    g = jnp.where(pick, -jnp.inf, g)
                attended = (sel > 0) & valid
            else:
                attended = valid
            q_aug = jnp.concatenate([q, jnp.where(attended, 0.0, NEG).astype(BF16)], axis=-1)
            for j in range(qi):
                cols = slice(j * blk, (j + 1) * blk)
                sj = lax.dot_general(q_aug, kaug[cols, :], contract_last,
                                     preferred_element_type=F32)
                s_sc[buf, :, cols] = sj
                macc = jnp.maximum(macc, halves_max(sj))
        mx_sc[buf] = macc

    def outputs(qi):
        buf = qi % 2
        rows = slice(qi * blk, (qi + 1) * blk)
        m = jnp.max(mx_sc[buf], axis=-1, keepdims=True)
        for j in range(qi + 1):
            cols = slice(j * blk, (j + 1) * blk)
            p_sc[buf, :, cols] = jnp.exp2((s_sc[buf, :, cols] - m) * c2).astype(BF16)
        acc = jnp.dot(p_sc[buf, :, 0:(qi + 1) * blk], vaug[0:(qi + 1) * blk, :],
                      preferred_element_type=F32)
        y = acc[:, 0:HEAD_DIM] / acc[:, HEAD_DIM:]
        o_ref[rows, :] = (y * _sigmoid(gm_ref[rows, :].astype(F32))).astype(o_ref.dtype)

    scores(0)
    for qi in range(nblk):
        if qi + 1 < nblk:
            scores(qi + 1)
        outputs(qi)


def _moba(proj2, nb, seq, d):
    n = proj2.shape[0]
    hpd = d // HEAD_DIM
    col = lambda off: (lambda b, h: (b, off * hpd + h))
    return pl.pallas_call(
        functools.partial(_moba_kernel, seq=seq),
        grid=(nb, ATT_HEADS),
        in_specs=[pl.BlockSpec((seq, HEAD_DIM), col(0)),
                  pl.BlockSpec((seq, HEAD_DIM), col(1)),
                  pl.BlockSpec((seq, HEAD_DIM), col(2)),
                  pl.BlockSpec((seq, HEAD_DIM), col(3))],
        out_specs=pl.BlockSpec((seq, HEAD_DIM), lambda b, h: (b, h)),
        out_shape=jax.ShapeDtypeStruct((n, d), BF16),
        scratch_shapes=[pltpu.VMEM((seq, HEAD_DIM + LANES), BF16),
                        pltpu.VMEM((seq, 2 * HEAD_DIM), BF16),
                        pltpu.VMEM((seq, LANES), F32),
                        pltpu.VMEM((2, MOBA_BLOCK, seq), F32),
                        pltpu.VMEM((2, MOBA_BLOCK, seq), BF16),
                        pltpu.VMEM((2, MOBA_BLOCK, LANES), F32)],
        compiler_params=_params("parallel", "parallel"),
        name="moba",
    )(proj2, proj2, proj2, proj2)


def _out_proj_kernel(ya_ref, ys_ref, ym_ref, x_ref, w_ref, g_ref, b_ref, o_ref):
    merged = (ya_ref[...].astype(F32) + ys_ref[...].astype(F32) + ym_ref[...].astype(F32))
    mix = jnp.dot(merged.astype(BF16), w_ref[...], preferred_element_type=F32)
    o_ref[...] = _layer_norm(DEEPNORM_ALPHA * x_ref[...] + mix, g_ref[...], b_ref[...])


def _out_proj(ya, ys, ym, x2, w_bf, g, b, tm=512):
    n, d = x2.shape
    rowblk = pl.BlockSpec((tm, d), lambda i: (i, 0))
    full = lambda shape: pl.BlockSpec(shape, lambda i: (0,) * len(shape))
    return pl.pallas_call(
        _out_proj_kernel,
        grid=(n // tm,),
        in_specs=[rowblk, rowblk, rowblk, rowblk, full((d, d)), full((1, d)), full((1, d))],
        out_specs=rowblk,
        out_shape=jax.ShapeDtypeStruct((n, d), F32),
        compiler_params=_params("parallel"),
        name="out_proj",
    )(ya, ys, ym, x2, w_bf, g.reshape(1, d), b.reshape(1, d))


def _ffn_kernel(x_ref, xh_ref, wup_ref, cw_ref, cb_ref, wdn_ref, g_ref, b_ref, o_ref, act_sc,
                *, tm, tf, d_ff, seq):
    i = pl.program_id(0)
    halo = SUBLANES
    x = x_ref[...]
    xb = x.astype(BF16)
    seq_start = (i * tm) % seq == 0
    xh = jnp.where(seq_start, 0.0, xh_ref[...]).astype(BF16)
    for c in range(d_ff // tf):
        cols = slice(c * tf, (c + 1) * tf)
        wg = wup_ref[:, cols]
        hg = jnp.dot(xb, wg, preferred_element_type=F32)
        hh = jnp.dot(xh, wg, preferred_element_type=F32)
        hu = jnp.dot(xb, wup_ref[:, d_ff + c * tf:d_ff + (c + 1) * tf],
                     preferred_element_type=F32)
        ext = jnp.concatenate([hh, hg], axis=0)
        cw = cw_ref[:, cols]
        gate = cw[2:3] * hg + cb_ref[:, cols]
        for j in range(1, FFN_CONV):
            gate = gate + cw[2 - j:3 - j] * ext[halo - j:halo - j + tm, :]
        act_sc[:, cols] = (_gelu(gate) * hu).astype(BF16)
    acc = jnp.dot(act_sc[...], wdn_ref[...], preferred_element_type=F32)
    o_ref[...] = _layer_norm(DEEPNORM_ALPHA * x + acc, g_ref[...], b_ref[...])


def _ffn(x2, w_up_bf, conv_w, conv_b, w_dn_bf, g, b, seq, tm=1024, tf=256):
    n, d = x2.shape
    d_ff = w_dn_bf.shape[0]
    hb = tm // SUBLANES
    full = lambda shape: pl.BlockSpec(shape, lambda i: (0,) * len(shape),
                                      pipeline_mode=pl.Buffered(1))
    return pl.pallas_call(
        functools.partial(_ffn_kernel, tm=tm, tf=tf, d_ff=d_ff, seq=seq),
        grid=(n // tm,),
        in_specs=[pl.BlockSpec((tm, d), lambda i: (i, 0)),
                  pl.BlockSpec((SUBLANES, d), lambda i: (jnp.maximum(i * hb - 1, 0), 0)),
                  full((d, 2 * d_ff)), full((FFN_CONV, d_ff)), full((1, d_ff)),
                  full((d_ff, d)), full((1, d)), full((1, d))],
        out_specs=pl.BlockSpec((tm, d), lambda i: (i, 0)),
        out_shape=jax.ShapeDtypeStruct((n, d), F32),
        scratch_shapes=[pltpu.VMEM((tm, d_ff), BF16)],
        compiler_params=_params("parallel"),
        name="ffn",
    )(x2, x2, w_up_bf, conv_w, conv_b.reshape(1, d_ff), w_dn_bf, g.reshape(1, d), b.reshape(1, d))


def kernel(x, w_in, conv_rg_w, conv_rg_b, w_rgate, b_rgate, w_igate, b_igate, lru_lambda, sgu_ln_g, sgu_ln_b, w_spatial, b_spatial, w_out, ln_mix_g, ln_mix_b, w_ffn_up, conv_ffn_w, conv_ffn_b, w_ffn_down, ln_ffn_g, ln_ffn_b):
    nb, seq, d = x.shape
    n = nb * seq
    depth = w_in.shape[0]
    x2 = x.reshape(n, d)
    for l in range(depth):
        wcol = lambda g: w_in[l][:, g * d:(g + 1) * d].astype(BF16)
        ya = _rglru(x2.reshape(nb, seq, d), wcol(0), wcol(1), wcol(7), conv_rg_w[l], conv_rg_b[l],
                    w_rgate[l], b_rgate[l], w_igate[l], b_igate[l], lru_lambda[l]).reshape(n, d)
        ys = _sgu(x2, wcol(2), wcol(3), wcol(8), sgu_ln_g[l], sgu_ln_b[l], w_spatial[l], b_spatial[l])
        w_att = jnp.concatenate([w_in[l][:, 4 * d:7 * d], w_in[l][:, 9 * d:10 * d]], axis=1).astype(BF16)
        ym = _moba(_in_proj(x2, w_att), nb, seq, d)
        x2 = _out_proj(ya, ys, ym, x2, w_out[l].astype(BF16), ln_mix_g[l], ln_mix_b[l])
        x2 = _ffn(x2, w_ffn_up[l].astype(BF16), conv_ffn_w[l], conv_ffn_b[l],
                  w_ffn_down[l].astype(BF16), ln_ffn_g[l], ln_ffn_b[l], seq)
    return x2.reshape(nb, seq, d)
```

```python
import functools
import math

import jax
import jax.numpy as jnp
from jax import lax
from jax.experimental import pallas as pl
from jax.experimental.pallas import tpu as pltpu

F32 = jnp.float32
BF16 = jnp.bfloat16

RG_CONV = 4
RG_C = 8.0
SG_GROUPS = 8
SG_CHUNK = 128
ATT_HEADS = 8
HEAD_DIM = 128
MOBA_BLOCK = 256
MOBA_TOPK = 3
FFN_CONV = 3
LN_EPS = 1e-5
DEPTH = 2
DEEPNORM_ALPHA = (2 * DEPTH) ** 0.25
NEG = -1e30

LANES = 128
SUBLANES = 8
VMEM_LIMIT = 56 * 1024 * 1024

_SQRT_2_OVER_PI = math.sqrt(2.0 / math.pi)


def _gelu(x):
    return x * (0.5 * (1.0 + jnp.tanh(_SQRT_2_OVER_PI * (x + 0.044715 * (x * x * x)))))


def _sigmoid(x):
    return 0.5 * jnp.tanh(0.5 * x) + 0.5


def _sqrt_nonneg(y):
    return jnp.where(y > 0.0, y * lax.rsqrt(y), 0.0)


def _layer_norm(y, g, b):
    mu = jnp.mean(y, axis=-1, keepdims=True)
    d = y - mu
    var = jnp.mean(d * d, axis=-1, keepdims=True)
    return d * lax.rsqrt(var + LN_EPS) * g + b


def _params(*sem):
    return pltpu.CompilerParams(dimension_semantics=sem, vmem_limit_bytes=VMEM_LIMIT)


def _in_proj_kernel(x_ref, w_ref, o_ref):
    o_ref[...] = jnp.dot(x_ref[...].astype(BF16), w_ref[...],
                         preferred_element_type=F32).astype(o_ref.dtype)


def _in_proj(x2, w_bf, tm=1024, tn=2048):
    n, d = x2.shape
    d_in = w_bf.shape[1]
    return pl.pallas_call(
        _in_proj_kernel,
        grid=(n // tm, d_in // tn),
        in_specs=[pl.BlockSpec((tm, d), lambda i, j: (i, 0)),
                  pl.BlockSpec((d, tn), lambda i, j: (0, j))],
        out_specs=pl.BlockSpec((tm, tn), lambda i, j: (i, j)),
        out_shape=jax.ShapeDtypeStruct((n, d_in), BF16),
        compiler_params=_params("parallel", "arbitrary"),
        name="in_proj",
    )(x2, w_bf)


def _rglru_kernel(x_ref, wx_ref, wg_ref, wm_ref, cw_ref, cb_ref, wri_ref, br_ref, bi_ref,
                  lam_ref, o_ref, xbuf, xc_sc, g_sc, a_sc, u_sc, hcarry, *, nb, t, cbw, sp):
    i = pl.program_id(0)
    halo = SUBLANES
    d = x_ref.shape[-1]
    nblk = d // LANES

    @pl.when(i == 0)
    def _():
        xbuf[:, 0:halo, :] = jnp.zeros((nb, halo, d), F32)
        hcarry[...] = jnp.zeros_like(hcarry)

    xb = x_ref[...].reshape(nb * t, d).astype(BF16)
    neg_lam = -lam_ref[...]
    softplus = jnp.maximum(neg_lam, 0.0) + jnp.log1p(jnp.exp(-jnp.abs(neg_lam)))
    decay_rate = -RG_C * softplus

    for c in range(d // cbw):
        cols = slice(c * cbw, (c + 1) * cbw)
        ax = jnp.dot(xb, wx_ref[:, cols], preferred_element_type=F32)
        cw = cw_ref[:, cols]
        for b in range(nb):
            x = ax[b * t:(b + 1) * t, :]
            xbuf[b, halo:halo + t, cols] = x
            xc = cw[3:4] * x + cb_ref[:, cols]
            for j in range(1, RG_CONV):
                xc = xc + cw[3 - j:4 - j] * xbuf[b, halo - j:halo - j + t, cols]
            xbuf[b, 0:halo, cols] = xbuf[b, t:t + halo, cols]
            xc_sc[b * t:(b + 1) * t, cols] = xc
        g_sc[:, cols] = (_gelu(jnp.dot(xb, wg_ref[:, cols], preferred_element_type=F32))
                         * _sigmoid(jnp.dot(xb, wm_ref[:, cols], preferred_element_type=F32))
                         ).astype(BF16)

    for k in range(nblk):
        lanes = slice(k * LANES, (k + 1) * LANES)
        xc = xc_sc[:, lanes]
        logits = jnp.dot(xc.astype(BF16), wri_ref[k].astype(BF16),
                         preferred_element_type=F32)
        r = _sigmoid(logits[:, 0:LANES] + br_ref[:, lanes])
        ig = _sigmoid(logits[:, LANES:] + bi_ref[:, lanes])
        a = jnp.exp(decay_rate[:, lanes] * r)
        u = _sqrt_nonneg(1.0 - a * a) * (ig * xc)
        for b in range(nb):
            a_sc[k, b * sp:b * sp + t, :] = a[b * t:(b + 1) * t, :]
            u_sc[k, b * sp:b * sp + t, :] = u[b * t:(b + 1) * t, :]

    hs = [hcarry[k] for k in range(nblk)]
    for p in range(t):
        rows = pl.ds(p, nb, stride=sp)
        for k in range(nblk):
            hs[k] = a_sc[k, rows, :] * hs[k] + u_sc[k, rows, :]
            u_sc[k, rows, :] = hs[k]
    for k in range(nblk):
        hcarry[k] = hs[k]

    for b in range(nb):
        h = jnp.concatenate([u_sc[k, b * sp:b * sp + t, :] for k in range(nblk)], axis=-1)
        o_ref[b] = (h * g_sc[b * t:(b + 1) * t, :].astype(F32)).astype(o_ref.dtype)


def _rglru(x3, wx, wg, wm, conv_w, conv_b, w_r, b_r, w_i, b_i, lam, t=128, cbw=256):
    nb, s, d = x3.shape
    sp = t + SUBLANES
    row = lambda a: a.reshape(1, d)
    w_ri = jnp.concatenate([w_r, w_i], axis=-1)
    full = lambda shape: pl.BlockSpec(shape, lambda i: (0,) * len(shape),
                                      pipeline_mode=pl.Buffered(1))
    return pl.pallas_call(
        functools.partial(_rglru_kernel, nb=nb, t=t, cbw=cbw, sp=sp),
        grid=(s // t,),
        in_specs=[pl.BlockSpec((nb, t, d), lambda i: (0, i, 0)),
                  full((d, d)), full((d, d)), full((d, d)),
                  full((RG_CONV, d)), full((1, d)),
                  full(w_ri.shape), full((1, d)), full((1, d)), full((1, d))],
        out_specs=pl.BlockSpec((nb, t, d), lambda i: (0, i, 0)),
        out_shape=jax.ShapeDtypeStruct((nb, s, d), BF16),
        scratch_shapes=[pltpu.VMEM((nb, t + SUBLANES, d), F32),
                        pltpu.VMEM((nb * t, d), F32),
                        pltpu.VMEM((nb * t, d), BF16),
                        pltpu.VMEM((d // LANES, nb * sp, LANES), F32),
                        pltpu.VMEM((d // LANES, nb * sp, LANES), F32),
                        pltpu.VMEM((d // LANES, nb, LANES), F32)],
        compiler_params=_params("arbitrary"),
        name="rglru",
    )(x3, wx, wg, wm, conv_w, row(conv_b), w_ri, row(b_r), row(b_i), row(lam))


def _sgu_kernel(x_ref, ya_ref, ym_ref, wu_ref, wv_ref, wg_ref, lg_ref, lb_ref, ws_ref, bias_ref,
                wo_ref, og_ref, ob_ref, o_ref, v_sc, m_sc, *, rows):
    x = x_ref[...]
    xb = x.astype(BF16)
    sv = jnp.dot(xb, wv_ref[...], preferred_element_type=F32)
    v_sc[...] = _layer_norm(_gelu(sv), lg_ref[...], lb_ref[...]).astype(BF16)
    tri = (lax.broadcasted_iota(jnp.int32, (SG_CHUNK, SG_CHUNK), 1)
           <= lax.broadcasted_iota(jnp.int32, (SG_CHUNK, SG_CHUNK), 0))
    pair = 2 * LANES
    for gp in range(SG_GROUPS // 2):
        pcols = slice(gp * pair, (gp + 1) * pair)
        u = (_gelu(jnp.dot(xb, wu_ref[:, pcols], preferred_element_type=F32))
             * _sigmoid(jnp.dot(xb, wg_ref[:, pcols], preferred_element_type=F32)))
        for gl in range(2):
            g = 2 * gp + gl
            w = jnp.where(tri, ws_ref[g], 0.0).astype(BF16)
            cols = slice(g * LANES, (g + 1) * LANES)
            for c in range(rows // SG_CHUNK):
                rs = slice(c * SG_CHUNK, (c + 1) * SG_CHUNK)
                mixed = (jnp.dot(w, v_sc[rs, cols], preferred_element_type=F32)
                         + bias_ref[:, cols])
                merged = (u[rs, gl * LANES:(gl + 1) * LANES] * mixed
                          + ya_ref[rs, cols].astype(F32) + ym_ref[rs, cols].astype(F32))
                m_sc[rs, cols] = merged.astype(BF16)
    mix = jnp.dot(m_sc[...], wo_ref[...], preferred_element_type=F32)
    o_ref[...] = _layer_norm(DEEPNORM_ALPHA * x + mix, og_ref[...], ob_ref[...])


def _sgu_merge(x2, ya, ym, wu, wv, wg, ln_g, ln_b, w_s, b_s, w_out, out_g, out_b, rows=1024):
    n, d = x2.shape
    bias = jnp.repeat(b_s.T, d // SG_GROUPS, axis=1)
    rowblk = pl.BlockSpec((rows, d), lambda r: (r, 0))
    full = lambda shape: pl.BlockSpec(shape, lambda r: (0,) * len(shape),
                                      pipeline_mode=pl.Buffered(1))
    return pl.pallas_call(
        functools.partial(_sgu_kernel, rows=rows),
        grid=(n // rows,),
        in_specs=[rowblk, rowblk, rowblk,
                  full((d, d)), full((d, d)), full((d, d)),
                  full((1, d)), full((1, d)),
                  full((SG_GROUPS, SG_CHUNK, SG_CHUNK)), full((SG_CHUNK, d)),
                  full((d, d)), full((1, d)), full((1, d))],
        out_specs=rowblk,
        out_shape=jax.ShapeDtypeStruct((n, d), F32),
        scratch_shapes=[pltpu.VMEM((rows, d), BF16),
                        pltpu.VMEM((rows, d), BF16)],
        compiler_params=_params("parallel"),
        name="sgu_merge",
    )(x2, ya, ym, wu, wv, wg, ln_g.reshape(1, d), ln_b.reshape(1, d), w_s, bias,
      w_out, out_g.reshape(1, d), out_b.reshape(1, d))


def _moba_kernel(q_ref, k_ref, v_ref, gm_ref, o_ref, kaug, vaug, gate_sc, s_sc, p_sc, mx_sc, *, seq):
    blk = MOBA_BLOCK
    nblk = seq // blk
    c2 = (HEAD_DIM ** -0.5) * math.log2(math.e)
    contract_last = (((1,), (1,)), ((), ()))

    row = lax.broadcasted_iota(jnp.int32, (LANES, seq), 0)
    pos = lax.broadcasted_iota(jnp.int32, (LANES, seq), 1)
    ind = jnp.where((pos >= row * blk) & (pos < (row + 1) * blk), 1.0, 0.0).astype(BF16)
    kmean = (jnp.dot(ind, k_ref[...], preferred_element_type=F32) * (1.0 / blk)).astype(BF16)
    gate_sc[...] = lax.dot_general(q_ref[...], kmean, contract_last, preferred_element_type=F32)
    kpos_all = lax.broadcasted_iota(jnp.int32, (seq, LANES), 0)
    kblk_all = lax.broadcasted_iota(jnp.int32, (seq, LANES), 1) * blk
    kaug[:, 0:HEAD_DIM] = k_ref[...]
    kaug[:, HEAD_DIM:] = jnp.where((kpos_all >= kblk_all) & (kpos_all < kblk_all + blk),
                                   1.0, 0.0).astype(BF16)
    vaug[:, 0:HEAD_DIM] = v_ref[...]
    vaug[:, HEAD_DIM:] = jnp.ones((seq, HEAD_DIM), BF16)

    lane = lax.broadcasted_iota(jnp.int32, (blk, LANES), 1)
    qpos = lax.broadcasted_iota(jnp.int32, (blk, blk), 0)
    kpos = lax.broadcasted_iota(jnp.int32, (blk, blk), 1)
    n_sel = min(MOBA_TOPK, nblk)

    def halves_max(s):
        return jnp.maximum(s[:, 0:LANES], s[:, LANES:2 * LANES])

    def scores(qi):
        buf = qi % 2
        rows = slice(qi * blk, (qi + 1) * blk)
        q = q_ref[rows, :]
        s_own = lax.dot_general(q, k_ref[rows, :], contract_last, preferred_element_type=F32)
        s_own = jnp.where(kpos <= qpos, s_own, NEG)
        s_sc[buf, :, rows] = s_own
        macc = halves_max(s_own)
        if qi > 0:
            valid = lane < qi
            if qi > n_sel:
                g = jnp.where(valid, gate_sc[rows, :], -jnp.inf)
                sel = jnp.zeros((blk, LANES), jnp.int32)
                for _ in range(n_sel):
                    gm = jnp.max(g, axis=-1, keepdims=True)
                    idx = jnp.min(jnp.where(g == gm, lane, LANES), axis=-1, keepdims=True)
                    pick = lane == idx
                    sel = jnp.where(pick, 1, sel)
                    g = jnp.where(pick, -jnp.inf, g)
                attended = (sel > 0) & valid
            else:
                attended = valid
            q_aug = jnp.concatenate([q, jnp.where(attended, 0.0, NEG).astype(BF16)], axis=-1)
            for j in range(qi):
                cols = slice(j * blk, (j + 1) * blk)
                sj = lax.dot_general(q_aug, kaug[cols, :], contract_last,
                                     preferred_element_type=F32)
                s_sc[buf, :, cols] = sj
                macc = jnp.maximum(macc, halves_max(sj))
        mx_sc[buf] = macc

    def outputs(qi):
        buf = qi % 2
        rows = slice(qi * blk, (qi + 1) * blk)
        m = jnp.max(mx_sc[buf], axis=-1, keepdims=True)
        for j in range(qi + 1):
            cols = slice(j * blk, (j + 1) * blk)
            p_sc[buf, :, cols] = jnp.exp2((s_sc[buf, :, cols] - m) * c2).astype(BF16)
        acc = jnp.dot(p_sc[buf, :, 0:(qi + 1) * blk], vaug[0:(qi + 1) * blk, :],
                      preferred_element_type=F32)
        y = acc[:, 0:HEAD_DIM] / acc[:, HEAD_DIM:]
        o_ref[rows, :] = (y * _sigmoid(gm_ref[rows, :].astype(F32))).astype(o_ref.dtype)

    scores(0)
    for qi in range(nblk):
        if qi + 1 < nblk:
            scores(qi + 1)
        outputs(qi)


def _moba(proj2, nb, seq, d):
    n = proj2.shape[0]
    hpd = d // HEAD_DIM
    col = lambda off: (lambda b, h: (b, off * hpd + h))
    return pl.pallas_call(
        functools.partial(_moba_kernel, seq=seq),
        grid=(nb, ATT_HEADS),
        in_specs=[pl.BlockSpec((seq, HEAD_DIM), col(0)),
                  pl.BlockSpec((seq, HEAD_DIM), col(1)),
                  pl.BlockSpec((seq, HEAD_DIM), col(2)),
                  pl.BlockSpec((seq, HEAD_DIM), col(3))],
        out_specs=pl.BlockSpec((seq, HEAD_DIM), lambda b, h: (b, h)),
        out_shape=jax.ShapeDtypeStruct((n, d), BF16),
        scratch_shapes=[pltpu.VMEM((seq, HEAD_DIM + LANES), BF16),
                        pltpu.VMEM((seq, 2 * HEAD_DIM), BF16),
                        pltpu.VMEM((seq, LANES), F32),
                        pltpu.VMEM((2, MOBA_BLOCK, seq), F32),
                        pltpu.VMEM((2, MOBA_BLOCK, seq), BF16),
                        pltpu.VMEM((2, MOBA_BLOCK, LANES), F32)],
        compiler_params=_params("parallel", "parallel"),
        name="moba",
    )(proj2, proj2, proj2, proj2)


def _ffn_kernel(x_ref, xh_ref, wup_ref, cw_ref, cb_ref, wdn_ref, g_ref, b_ref, o_ref, act_sc,
                *, tm, tf, d_ff, seq):
    i = pl.program_id(0)
    halo = SUBLANES
    x = x_ref[...]
    xb = x.astype(BF16)
    seq_start = (i * tm) % seq == 0
    xh = jnp.where(seq_start, 0.0, xh_ref[...]).astype(BF16)
    for c in range(d_ff // tf):
        cols = slice(c * tf, (c + 1) * tf)
        wg = wup_ref[:, cols]
        hg = jnp.dot(xb, wg, preferred_element_type=F32)
        hh = jnp.dot(xh, wg, preferred_element_type=F32)
        hu = jnp.dot(xb, wup_ref[:, d_ff + c * tf:d_ff + (c + 1) * tf],
                     preferred_element_type=F32)
        ext = jnp.concatenate([hh, hg], axis=0)
        cw = cw_ref[:, cols]
        gate = cw[2:3] * hg + cb_ref[:, cols]
        for j in range(1, FFN_CONV):
            gate = gate + cw[2 - j:3 - j] * ext[halo - j:halo - j + tm, :]
        act_sc[:, cols] = (_gelu(gate) * hu).astype(BF16)
    acc = jnp.dot(act_sc[...], wdn_ref[...], preferred_element_type=F32)
    o_ref[...] = _layer_norm(DEEPNORM_ALPHA * x + acc, g_ref[...], b_ref[...])


def _ffn(x2, w_up_bf, conv_w, conv_b, w_dn_bf, g, b, seq, tm=1024, tf=256):
    n, d = x2.shape
    d_ff = w_dn_bf.shape[0]
    hb = tm // SUBLANES
    full = lambda shape: pl.BlockSpec(shape, lambda i: (0,) * len(shape),
                                      pipeline_mode=pl.Buffered(1))
    return pl.pallas_call(
        functools.partial(_ffn_kernel, tm=tm, tf=tf, d_ff=d_ff, seq=seq),
        grid=(n // tm,),
        in_specs=[pl.BlockSpec((tm, d), lambda i: (i, 0)),
                  pl.BlockSpec((SUBLANES, d), lambda i: (jnp.maximum(i * hb - 1, 0), 0)),
                  full((d, 2 * d_ff)), full((FFN_CONV, d_ff)), full((1, d_ff)),
                  full((d_ff, d)), full((1, d)), full((1, d))],
        out_specs=pl.BlockSpec((tm, d), lambda i: (i, 0)),
        out_shape=jax.ShapeDtypeStruct((n, d), F32),
        scratch_shapes=[pltpu.VMEM((tm, d_ff), BF16)],
        compiler_params=_params("parallel"),
        name="ffn",
    )(x2, x2, w_up_bf, conv_w, conv_b.reshape(1, d_ff), w_dn_bf, g.reshape(1, d), b.reshape(1, d))


def kernel(x, w_in, conv_rg_w, conv_rg_b, w_rgate, b_rgate, w_igate, b_igate, lru_lambda, sgu_ln_g, sgu_ln_b, w_spatial, b_spatial, w_out, ln_mix_g, ln_mix_b, w_ffn_up, conv_ffn_w, conv_ffn_b, w_ffn_down, ln_ffn_g, ln_ffn_b):
    nb, seq, d = x.shape
    n = nb * seq
    depth = w_in.shape[0]
    x2 = x.reshape(n, d)
    for l in range(depth):
        wcol = lambda g: w_in[l][:, g * d:(g + 1) * d].astype(BF16)
        ya = _rglru(x2.reshape(nb, seq, d), wcol(0), wcol(1), wcol(7), conv_rg_w[l], conv_rg_b[l],
                    w_rgate[l], b_rgate[l], w_igate[l], b_igate[l], lru_lambda[l]).reshape(n, d)
        w_att = jnp.concatenate([w_in[l][:, 4 * d:7 * d], w_in[l][:, 9 * d:10 * d]], axis=1).astype(BF16)
        ym = _moba(_in_proj(x2, w_att), nb, seq, d)
        x2 = _sgu_merge(x2, ya, ym, wcol(2), wcol(3), wcol(8), sgu_ln_g[l], sgu_ln_b[l], w_spatial[l],
                        b_spatial[l], w_out[l].astype(BF16), ln_mix_g[l], ln_mix_b[l])
        x2 = _ffn(x2, w_ffn_up[l].astype(BF16), conv_ffn_w[l], conv_ffn_b[l],
                  w_ffn_down[l].astype(BF16), ln_ffn_g[l], ln_ffn_b[l], seq)
    return x2.reshape(nb, seq, d)
```

```python
import functools
import math

import jax
import jax.numpy as jnp
from jax import lax
from jax.experimental import pallas as pl
from jax.experimental.pallas import tpu as pltpu

F32 = jnp.float32
BF16 = jnp.bfloat16

RG_CONV = 4
RG_C = 8.0
SG_GROUPS = 8
SG_CHUNK = 128
ATT_HEADS = 8
HEAD_DIM = 128
MOBA_BLOCK = 256
MOBA_TOPK = 3
FFN_CONV = 3
LN_EPS = 1e-5
DEPTH = 2
DEEPNORM_ALPHA = (2 * DEPTH) ** 0.25
NEG = -1e30

LANES = 128
SUBLANES = 8
VMEM_LIMIT = 56 * 1024 * 1024

_SQRT_2_OVER_PI = math.sqrt(2.0 / math.pi)


def _gelu(x):
    return x * (0.5 * (1.0 + jnp.tanh(_SQRT_2_OVER_PI * (x + 0.044715 * (x * x * x)))))


def _sigmoid(x):
    return 0.5 * jnp.tanh(0.5 * x) + 0.5


def _sqrt_nonneg(y):
    return jnp.where(y > 0.0, y * lax.rsqrt(y), 0.0)


def _layer_norm(y, g, b):
    mu = jnp.mean(y, axis=-1, keepdims=True)
    d = y - mu
    var = jnp.mean(d * d, axis=-1, keepdims=True)
    return d * lax.rsqrt(var + LN_EPS) * g + b


def _params(*sem):
    return pltpu.CompilerParams(dimension_semantics=sem, vmem_limit_bytes=VMEM_LIMIT)


def _in_proj_kernel(x_ref, w_ref, o_ref):
    o_ref[...] = jnp.dot(x_ref[...].astype(BF16), w_ref[...],
                         preferred_element_type=F32).astype(o_ref.dtype)


def _in_proj(x2, w_bf, tm=1024, tn=2048):
    n, d = x2.shape
    d_in = w_bf.shape[1]
    return pl.pallas_call(
        _in_proj_kernel,
        grid=(n // tm, d_in // tn),
        in_specs=[pl.BlockSpec((tm, d), lambda i, j: (i, 0)),
                  pl.BlockSpec((d, tn), lambda i, j: (0, j))],
        out_specs=pl.BlockSpec((tm, tn), lambda i, j: (i, j)),
        out_shape=jax.ShapeDtypeStruct((n, d_in), BF16),
        compiler_params=_params("parallel", "arbitrary"),
        name="in_proj",
    )(x2, w_bf)


def _rglru_kernel(x_ref, wx_ref, wg_ref, wm_ref, cw_ref, cb_ref, wri_ref, br_ref, bi_ref,
                  lam_ref, o_ref, xbuf, xc_sc, g_sc, a_sc, u_sc, hcarry, *, nb, t, cbw, sp):
    i = pl.program_id(0)
    halo = SUBLANES
    d = x_ref.shape[-1]
    nblk = d // LANES

    @pl.when(i == 0)
    def _():
        xbuf[:, 0:halo, :] = jnp.zeros((nb, halo, d), F32)
        hcarry[...] = jnp.zeros_like(hcarry)

    xb = x_ref[...].reshape(nb * t, d).astype(BF16)
    neg_lam = -lam_ref[...]
    softplus = jnp.maximum(neg_lam, 0.0) + jnp.log1p(jnp.exp(-jnp.abs(neg_lam)))
    decay_rate = -RG_C * softplus

    for c in range(d // cbw):
        cols = slice(c * cbw, (c + 1) * cbw)
        ax = jnp.dot(xb, wx_ref[:, cols], preferred_element_type=F32)
        cw = cw_ref[:, cols]
        for b in range(nb):
            x = ax[b * t:(b + 1) * t, :]
            xbuf[b, halo:halo + t, cols] = x
            xc = cw[3:4] * x + cb_ref[:, cols]
            for j in range(1, RG_CONV):
                xc = xc + cw[3 - j:4 - j] * xbuf[b, halo - j:halo - j + t, cols]
            xbuf[b, 0:halo, cols] = xbuf[b, t:t + halo, cols]
            xc_sc[b * t:(b + 1) * t, cols] = xc
        g_sc[:, cols] = (_gelu(jnp.dot(xb, wg_ref[:, cols], preferred_element_type=F32))
                         * _sigmoid(jnp.dot(xb, wm_ref[:, cols], preferred_element_type=F32))
                         ).astype(BF16)

    half_rate = (0.5 * math.log2(math.e)) * decay_rate
    for k in range(nblk):
        lanes = slice(k * LANES, (k + 1) * LANES)
        xc = xc_sc[:, lanes]
        half_logits = jnp.dot(xc.astype(BF16), wri_ref[k].astype(BF16),
                              preferred_element_type=F32)
        t_r = jnp.tanh(half_logits[:, 0:LANES] + br_ref[:, lanes])
        t_i = jnp.tanh(half_logits[:, LANES:] + bi_ref[:, lanes])
        a = jnp.exp2(half_rate[:, lanes] * t_r + half_rate[:, lanes])
        xc_half = 0.5 * xc
        u = _sqrt_nonneg(1.0 - a * a) * (t_i * xc_half + xc_half)
        for b in range(nb):
            a_sc[k, b * sp:b * sp + t, :] = a[b * t:(b + 1) * t, :]
            u_sc[k, b * sp:b * sp + t, :] = u[b * t:(b + 1) * t, :]

    hs = [hcarry[k] for k in range(nblk)]
    for p in range(t):
        rows = pl.ds(p, nb, stride=sp)
        for k in range(nblk):
            hs[k] = a_sc[k, rows, :] * hs[k] + u_sc[k, rows, :]
            u_sc[k, rows, :] = hs[k]
    for k in range(nblk):
        hcarry[k] = hs[k]

    for b in range(nb):
        h = jnp.concatenate([u_sc[k, b * sp:b * sp + t, :] for k in range(nblk)], axis=-1)
        o_ref[b] = (h * g_sc[b * t:(b + 1) * t, :].astype(F32)).astype(o_ref.dtype)


def _rglru(x3, wx, wg, wm, conv_w, conv_b, w_r, b_r, w_i, b_i, lam, t=128, cbw=256):
    nb, s, d = x3.shape
    sp = t + SUBLANES
    row = lambda a: a.reshape(1, d)
    w_ri = 0.5 * jnp.concatenate([w_r, w_i], axis=-1)
    b_r, b_i = 0.5 * b_r, 0.5 * b_i
    full = lambda shape: pl.BlockSpec(shape, lambda i: (0,) * len(shape),
                                      pipeline_mode=pl.Buffered(1))
    return pl.pallas_call(
        functools.partial(_rglru_kernel, nb=nb, t=t, cbw=cbw, sp=sp),
        grid=(s // t,),
        in_specs=[pl.BlockSpec((nb, t, d), lambda i: (0, i, 0)),
                  full((d, d)), full((d, d)), full((d, d)),
                  full((RG_CONV, d)), full((1, d)),
                  full(w_ri.shape), full((1, d)), full((1, d)), full((1, d))],
        out_specs=pl.BlockSpec((nb, t, d), lambda i: (0, i, 0)),
        out_shape=jax.ShapeDtypeStruct((nb, s, d), BF16),
        scratch_shapes=[pltpu.VMEM((nb, t + SUBLANES, d), F32),
                        pltpu.VMEM((nb * t, d), F32),
                        pltpu.VMEM((nb * t, d), BF16),
                        pltpu.VMEM((d // LANES, nb * sp, LANES), F32),
                        pltpu.VMEM((d // LANES, nb * sp, LANES), F32),
                        pltpu.VMEM((d // LANES, nb, LANES), F32)],
        compiler_params=_params("arbitrary"),
        name="rglru",
    )(x3, wx, wg, wm, conv_w, row(conv_b), w_ri, row(b_r), row(b_i), row(lam))


def _sgu_kernel(x_ref, ya_ref, ym_ref, wu_ref, wv_ref, wg_ref, lg_ref, lb_ref, ws_ref, bias_ref,
                wo_ref, og_ref, ob_ref, o_ref, v_sc, m_sc, *, rows):
    x = x_ref[...]
    xb = x.astype(BF16)
    sv = jnp.dot(xb, wv_ref[...], preferred_element_type=F32)
    v_sc[...] = _layer_norm(_gelu(sv), lg_ref[...], lb_ref[...]).astype(BF16)
    tri = (lax.broadcasted_iota(jnp.int32, (SG_CHUNK, SG_CHUNK), 1)
           <= lax.broadcasted_iota(jnp.int32, (SG_CHUNK, SG_CHUNK), 0))
    pair = 2 * LANES
    for gp in range(SG_GROUPS // 2):
        pcols = slice(gp * pair, (gp + 1) * pair)
        u = (_gelu(jnp.dot(xb, wu_ref[:, pcols], preferred_element_type=F32))
             * _sigmoid(jnp.dot(xb, wg_ref[:, pcols], preferred_element_type=F32)))
        for gl in range(2):
            g = 2 * gp + gl
            w = jnp.where(tri, ws_ref[g], 0.0).astype(BF16)
            cols = slice(g * LANES, (g + 1) * LANES)
            for c in range(rows // SG_CHUNK):
                rs = slice(c * SG_CHUNK, (c + 1) * SG_CHUNK)
                mixed = (jnp.dot(w, v_sc[rs, cols], preferred_element_type=F32)
                         + bias_ref[:, cols])
                merged = (u[rs, gl * LANES:(gl + 1) * LANES] * mixed
                          + ya_ref[rs, cols].astype(F32) + ym_ref[rs, cols].astype(F32))
                m_sc[rs, cols] = merged.astype(BF16)
    mix = jnp.dot(m_sc[...], wo_ref[...], preferred_element_type=F32)
    o_ref[...] = _layer_norm(DEEPNORM_ALPHA * x + mix, og_ref[...], ob_ref[...])


def _sgu_merge(x2, ya, ym, wu, wv, wg, ln_g, ln_b, w_s, b_s, w_out, out_g, out_b, rows=1024):
    n, d = x2.shape
    bias = jnp.repeat(b_s.T, d // SG_GROUPS, axis=1)
    rowblk = pl.BlockSpec((rows, d), lambda r: (r, 0))
    full = lambda shape: pl.BlockSpec(shape, lambda r: (0,) * len(shape),
                                      pipeline_mode=pl.Buffered(1))
    return pl.pallas_call(
        functools.partial(_sgu_kernel, rows=rows),
        grid=(n // rows,),
        in_specs=[rowblk, rowblk, rowblk,
                  full((d, d)), full((d, d)), full((d, d)),
                  full((1, d)), full((1, d)),
                  full((SG_GROUPS, SG_CHUNK, SG_CHUNK)), full((SG_CHUNK, d)),
                  full((d, d)), full((1, d)), full((1, d))],
        out_specs=rowblk,
        out_shape=jax.ShapeDtypeStruct((n, d), F32),
        scratch_shapes=[pltpu.VMEM((rows, d), BF16),
                        pltpu.VMEM((rows, d), BF16)],
        compiler_params=_params("parallel"),
        name="sgu_merge",
    )(x2, ya, ym, wu, wv, wg, ln_g.reshape(1, d), ln_b.reshape(1, d), w_s, bias,
      w_out, out_g.reshape(1, d), out_b.reshape(1, d))


def _moba_kernel(q_ref, k_ref, v_ref, gm_ref, o_ref, kaug, vaug, gate_sc, s_sc, p_sc, mx_sc, *, seq):
    blk = MOBA_BLOCK
    nblk = seq // blk
    c2 = (HEAD_DIM ** -0.5) * math.log2(math.e)
    contract_last = (((1,), (1,)), ((), ()))

    row = lax.broadcasted_iota(jnp.int32, (LANES, seq), 0)
    pos = lax.broadcasted_iota(jnp.int32, (LANES, seq), 1)
    ind = jnp.where((pos >= row * blk) & (pos < (row + 1) * blk), 1.0, 0.0).astype(BF16)
    kmean = (jnp.dot(ind, k_ref[...], preferred_element_type=F32) * (1.0 / blk)).astype(BF16)
    gate_sc[...] = lax.dot_general(q_ref[...], kmean, contract_last, preferred_element_type=F32)
    kpos_all = lax.broadcasted_iota(jnp.int32, (seq, LANES), 0)
    kblk_all = lax.broadcasted_iota(jnp.int32, (seq, LANES), 1) * blk
    kaug[:, 0:HEAD_DIM] = k_ref[...]
    kaug[:, HEAD_DIM:] = jnp.where((kpos_all >= kblk_all) & (kpos_all < kblk_all + blk),
                                   1.0, 0.0).astype(BF16)
    vaug[:, 0:HEAD_DIM] = v_ref[...]
    vaug[:, HEAD_DIM:] = jnp.ones((seq, HEAD_DIM), BF16)

    lane = lax.broadcasted_iota(jnp.int32, (blk, LANES), 1)
    qpos = lax.broadcasted_iota(jnp.int32, (blk, blk), 0)
    kpos = lax.broadcasted_iota(jnp.int32, (blk, blk), 1)
    n_sel = min(MOBA_TOPK, nblk)

    def halves_max(s):
        return jnp.maximum(s[:, 0:LANES], s[:, LANES:2 * LANES])

    def scores(qi):
        buf = qi % 2
        rows = slice(qi * blk, (qi + 1) * blk)
        q = q_ref[rows, :]
        s_own = lax.dot_general(q, k_ref[rows, :], contract_last, preferred_element_type=F32)
        s_own = jnp.where(kpos <= qpos, s_own, NEG)
        s_sc[buf, :, rows] = s_own
        macc = halves_max(s_own)
        if qi > 0:
            valid = lane < qi
            if qi > n_sel:
                g = jnp.where(valid, gate_sc[rows, :], -jnp.inf)
                sel = jnp.zeros((blk, LANES), jnp.int32)
                for _ in range(n_sel):
                    gm = jnp.max(g, axis=-1, keepdims=True)
                    idx = jnp.min(jnp.where(g == gm, lane, LANES), axis=-1, keepdims=True)
                    pick = lane == idx
                    sel = jnp.where(pick, 1, sel)
                    g = jnp.where(pick, -jnp.inf, g)
                attended = (sel > 0) & valid
            else:
                attended = valid
            q_aug = jnp.concatenate([q, jnp.where(attended, 0.0, NEG).astype(BF16)], axis=-1)
            for j in range(qi):
                cols = slice(j * blk, (j + 1) * blk)
                sj = lax.dot_general(q_aug, kaug[cols, :], contract_last,
                                     preferred_element_type=F32)
                s_sc[buf, :, cols] = sj
                macc = jnp.maximum(macc, halves_max(sj))
        mx_sc[buf] = macc

    def outputs(qi):
        buf = qi % 2
        rows = slice(qi * blk, (qi + 1) * blk)
        m = jnp.max(mx_sc[buf], axis=-1, keepdims=True)
        for j in range(qi + 1):
            cols = slice(j * blk, (j + 1) * blk)
            p_sc[buf, :, cols] = jnp.exp2((s_sc[buf, :, cols] - m) * c2).astype(BF16)
        acc = jnp.dot(p_sc[buf, :, 0:(qi + 1) * blk], vaug[0:(qi + 1) * blk, :],
                      preferred_element_type=F32)
        y = acc[:, 0:HEAD_DIM] / acc[:, HEAD_DIM:]
        o_ref[rows, :] = (y * _sigmoid(gm_ref[rows, :].astype(F32))).astype(o_ref.dtype)

    scores(0)
    for qi in range(nblk):
        if qi + 1 < nblk:
            scores(qi + 1)
        outputs(qi)


def _moba(proj2, nb, seq, d):
    n = proj2.shape[0]
    hpd = d // HEAD_DIM
    col = lambda off: (lambda b, h: (b, off * hpd + h))
    return pl.pallas_call(
        functools.partial(_moba_kernel, seq=seq),
        grid=(nb, ATT_HEADS),
        in_specs=[pl.BlockSpec((seq, HEAD_DIM), col(0)),
                  pl.BlockSpec((seq, HEAD_DIM), col(1)),
                  pl.BlockSpec((seq, HEAD_DIM), col(2)),
                  pl.BlockSpec((seq, HEAD_DIM), col(3))],
        out_specs=pl.BlockSpec((seq, HEAD_DIM), lambda b, h: (b, h)),
        out_shape=jax.ShapeDtypeStruct((n, d), BF16),
        scratch_shapes=[pltpu.VMEM((seq, HEAD_DIM + LANES), BF16),
                        pltpu.VMEM((seq, 2 * HEAD_DIM), BF16),
                        pltpu.VMEM((seq, LANES), F32),
                        pltpu.VMEM((2, MOBA_BLOCK, seq), F32),
                        pltpu.VMEM((2, MOBA_BLOCK, seq), BF16),
                        pltpu.VMEM((2, MOBA_BLOCK, LANES), F32)],
        compiler_params=_params("parallel", "parallel"),
        name="moba",
    )(proj2, proj2, proj2, proj2)


def _ffn_kernel(x_ref, xh_ref, wup_ref, cw_ref, cb_ref, wdn_ref, g_ref, b_ref, o_ref, act_sc,
                *, tm, tf, d_ff, seq):
    i = pl.program_id(0)
    halo = SUBLANES
    x = x_ref[...]
    xb = x.astype(BF16)
    seq_start = (i * tm) % seq == 0
    xh = jnp.where(seq_start, 0.0, xh_ref[...]).astype(BF16)
    for c in range(d_ff // tf):
        cols = slice(c * tf, (c + 1) * tf)
        wg = wup_ref[:, cols]
        hg = jnp.dot(xb, wg, preferred_element_type=F32)
        hh = jnp.dot(xh, wg, preferred_element_type=F32)
        hu = jnp.dot(xb, wup_ref[:, d_ff + c * tf:d_ff + (c + 1) * tf],
                     preferred_element_type=F32)
        ext = jnp.concatenate([hh, hg], axis=0)
        cw = cw_ref[:, cols]
        gate = cw[2:3] * hg + cb_ref[:, cols]
        for j in range(1, FFN_CONV):
            gate = gate + cw[2 - j:3 - j] * ext[halo - j:halo - j + tm, :]
        act_sc[:, cols] = (_gelu(gate) * hu).astype(BF16)
    acc = jnp.dot(act_sc[...], wdn_ref[...], preferred_element_type=F32)
    o_ref[...] = _layer_norm(DEEPNORM_ALPHA * x + acc, g_ref[...], b_ref[...])


def _ffn(x2, w_up_bf, conv_w, conv_b, w_dn_bf, g, b, seq, tm=1024, tf=256):
    n, d = x2.shape
    d_ff = w_dn_bf.shape[0]
    hb = tm // SUBLANES
    full = lambda shape: pl.BlockSpec(shape, lambda i: (0,) * len(shape),
                                      pipeline_mode=pl.Buffered(1))
    return pl.pallas_call(
        functools.partial(_ffn_kernel, tm=tm, tf=tf, d_ff=d_ff, seq=seq),
        grid=(n // tm,),
        in_specs=[pl.BlockSpec((tm, d), lambda i: (i, 0)),
                  pl.BlockSpec((SUBLANES, d), lambda i: (jnp.maximum(i * hb - 1, 0), 0)),
                  full((d, 2 * d_ff)), full((FFN_CONV, d_ff)), full((1, d_ff)),
                  full((d_ff, d)), full((1, d)), full((1, d))],
        out_specs=pl.BlockSpec((tm, d), lambda i: (i, 0)),
        out_shape=jax.ShapeDtypeStruct((n, d), F32),
        scratch_shapes=[pltpu.VMEM((tm, d_ff), BF16)],
        compiler_params=_params("parallel"),
        name="ffn",
    )(x2, x2, w_up_bf, conv_w, conv_b.reshape(1, d_ff), w_dn_bf, g.reshape(1, d), b.reshape(1, d))


def kernel(x, w_in, conv_rg_w, conv_rg_b, w_rgate, b_rgate, w_igate, b_igate, lru_lambda, sgu_ln_g, sgu_ln_b, w_spatial, b_spatial, w_out, ln_mix_g, ln_mix_b, w_ffn_up, conv_ffn_w, conv_ffn_b, w_ffn_down, ln_ffn_g, ln_ffn_b):
    nb, seq, d = x.shape
    n = nb * seq
    depth = w_in.shape[0]
    x2 = x.reshape(n, d)
    for l in range(depth):
        wcol = lambda g: w_in[l][:, g * d:(g + 1) * d].astype(BF16)
        ya = _rglru(x2.reshape(nb, seq, d), wcol(0), wcol(1), wcol(7), conv_rg_w[l], conv_rg_b[l],
                    w_rgate[l], b_rgate[l], w_igate[l], b_igate[l], lru_lambda[l]).reshape(n, d)
        w_att = jnp.concatenate([w_in[l][:, 4 * d:7 * d], w_in[l][:, 9 * d:10 * d]], axis=1).astype(BF16)
        ym = _moba(_in_proj(x2, w_att), nb, seq, d)
        x2 = _sgu_merge(x2, ya, ym, wcol(2), wcol(3), wcol(8), sgu_ln_g[l], sgu_ln_b[l], w_spatial[l],
                        b_spatial[l], w_out[l].astype(BF16), ln_mix_g[l], ln_mix_b[l])
        x2 = _ffn(x2, w_ffn_up[l].astype(BF16), conv_ffn_w[l], conv_ffn_b[l],
                  w_ffn_down[l].astype(BF16), ln_ffn_g[l], ln_ffn_b[l], seq)
    return x2.reshape(nb, seq, d)
```

```python
import functools
import math

import jax
import jax.numpy as jnp
from jax import lax
from jax.experimental import pallas as pl
from jax.experimental.pallas import tpu as pltpu

F32 = jnp.float32
BF16 = jnp.bfloat16

RG_CONV = 4
RG_C = 8.0
SG_GROUPS = 8
SG_CHUNK = 128
ATT_HEADS = 8
HEAD_DIM = 128
MOBA_BLOCK = 256
MOBA_TOPK = 3
FFN_CONV = 3
LN_EPS = 1e-5
DEPTH = 2
DEEPNORM_ALPHA = (2 * DEPTH) ** 0.25
NEG = -1e30

LANES = 128
SUBLANES = 8
BF16_ROWS = 16
VMEM_LIMIT = 56 * 1024 * 1024

_SQRT_2_OVER_PI = math.sqrt(2.0 / math.pi)


def _gelu(x):
    return x * (0.5 * (1.0 + jnp.tanh(_SQRT_2_OVER_PI * (x + 0.044715 * (x * x * x)))))


def _sigmoid(x):
    return 0.5 * jnp.tanh(0.5 * x) + 0.5


def _gelu_mul_sigmoid(x, half_z):
    inner = _SQRT_2_OVER_PI * (x + 0.044715 * (x * x * x))
    return (0.25 * x) * (1.0 + jnp.tanh(inner)) * (1.0 + jnp.tanh(half_z))


def _sqrt_nonneg(y):
    return jnp.where(y > 0.0, y * lax.rsqrt(y), 0.0)


def _layer_norm(y, g, b):
    mu = jnp.mean(y, axis=-1, keepdims=True)
    d = y - mu
    var = jnp.mean(d * d, axis=-1, keepdims=True)
    return d * lax.rsqrt(var + LN_EPS) * g + b


def _params(*sem):
    return pltpu.CompilerParams(dimension_semantics=sem, vmem_limit_bytes=VMEM_LIMIT)


def _in_proj_kernel(x_ref, w_ref, o_ref):
    o_ref[...] = jnp.dot(x_ref[...].astype(BF16), w_ref[...],
                         preferred_element_type=F32).astype(o_ref.dtype)


def _in_proj(x2, w_bf, tm=1024, tn=2048):
    n, d = x2.shape
    d_in = w_bf.shape[1]
    return pl.pallas_call(
        _in_proj_kernel,
        grid=(n // tm, d_in // tn),
        in_specs=[pl.BlockSpec((tm, d), lambda i, j: (i, 0)),
                  pl.BlockSpec((d, tn), lambda i, j: (0, j))],
        out_specs=pl.BlockSpec((tm, tn), lambda i, j: (i, j)),
        out_shape=jax.ShapeDtypeStruct((n, d_in), BF16),
        compiler_params=_params("parallel", "arbitrary"),
        name="in_proj",
    )(x2, w_bf)


def _rglru_kernel(x_ref, wx_ref, wg_ref, wm_ref, cw_ref, cb_ref, wri_ref, br_ref, bi_ref,
                  lam_ref, o_ref, xbuf, xc_sc, g_sc, a_sc, u_sc, hcarry, *, nb, t, cbw, sp):
    i = pl.program_id(0)
    halo = SUBLANES
    d = x_ref.shape[-1]
    nblk = d // LANES

    @pl.when(i == 0)
    def _():
        xbuf[:, 0:halo, :] = jnp.zeros((nb, halo, d), F32)
        hcarry[...] = jnp.zeros_like(hcarry)

    xb = x_ref[...].reshape(nb * t, d).astype(BF16)
    neg_lam = -lam_ref[...]
    softplus = jnp.maximum(neg_lam, 0.0) + jnp.log1p(jnp.exp(-jnp.abs(neg_lam)))
    decay_rate = -RG_C * softplus

    for c in range(d // cbw):
        cols = slice(c * cbw, (c + 1) * cbw)
        ax = jnp.dot(xb, wx_ref[:, cols], preferred_element_type=F32)
        cw = cw_ref[:, cols]
        for b in range(nb):
            x = ax[b * t:(b + 1) * t, :]
            xbuf[b, halo:halo + t, cols] = x
            xc = cw[3:4] * x + cb_ref[:, cols]
            for j in range(1, RG_CONV):
                xc = xc + cw[3 - j:4 - j] * xbuf[b, halo - j:halo - j + t, cols]
            xbuf[b, 0:halo, cols] = xbuf[b, t:t + halo, cols]
            xc_sc[b * t:(b + 1) * t, cols] = xc
        g_sc[:, cols] = _gelu_mul_sigmoid(
            jnp.dot(xb, wg_ref[:, cols], preferred_element_type=F32),
            jnp.dot(xb, wm_ref[:, cols], preferred_element_type=F32)).astype(BF16)

    half_rate = (0.5 * math.log2(math.e)) * decay_rate
    for k in range(nblk):
        lanes = slice(k * LANES, (k + 1) * LANES)
        xc = xc_sc[:, lanes]
        half_logits = jnp.dot(xc.astype(BF16), wri_ref[k].astype(BF16),
                              preferred_element_type=F32)
        t_r = jnp.tanh(half_logits[:, 0:LANES] + br_ref[:, lanes])
        t_i = jnp.tanh(half_logits[:, LANES:] + bi_ref[:, lanes])
        a = jnp.exp2(half_rate[:, lanes] * t_r + half_rate[:, lanes])
        xc_half = 0.5 * xc
        u = _sqrt_nonneg(1.0 - a * a) * (t_i * xc_half + xc_half)
        for b in range(nb):
            a_sc[k, b * sp:b * sp + t, :] = a[b * t:(b + 1) * t, :]
            u_sc[k, b * sp:b * sp + t, :] = u[b * t:(b + 1) * t, :]

    hs = [hcarry[k] for k in range(nblk)]
    for p in range(t):
        rows = pl.ds(p, nb, stride=sp)
        for k in range(nblk):
            hs[k] = a_sc[k, rows, :] * hs[k] + u_sc[k, rows, :]
            u_sc[k, rows, :] = hs[k]
    for k in range(nblk):
        hcarry[k] = hs[k]

    for b in range(nb):
        h = jnp.concatenate([u_sc[k, b * sp:b * sp + t, :] for k in range(nblk)], axis=-1)
        o_ref[b] = (h * g_sc[b * t:(b + 1) * t, :].astype(F32)).astype(o_ref.dtype)


def _rglru(x3, wx, wg, wm, conv_w, conv_b, w_r, b_r, w_i, b_i, lam, t=128, cbw=256):
    nb, s, d = x3.shape
    sp = t + SUBLANES
    row = lambda a: a.reshape(1, d)
    w_ri = 0.5 * jnp.concatenate([w_r, w_i], axis=-1)
    b_r, b_i = 0.5 * b_r, 0.5 * b_i
    full = lambda shape: pl.BlockSpec(shape, lambda i: (0,) * len(shape),
                                      pipeline_mode=pl.Buffered(1))
    return pl.pallas_call(
        functools.partial(_rglru_kernel, nb=nb, t=t, cbw=cbw, sp=sp),
        grid=(s // t,),
        in_specs=[pl.BlockSpec((nb, t, d), lambda i: (0, i, 0)),
                  full((d, d)), full((d, d)), full((d, d)),
                  full((RG_CONV, d)), full((1, d)),
                  full(w_ri.shape), full((1, d)), full((1, d)), full((1, d))],
        out_specs=pl.BlockSpec((nb, t, d), lambda i: (0, i, 0)),
        out_shape=jax.ShapeDtypeStruct((nb, s, d), BF16),
        scratch_shapes=[pltpu.VMEM((nb, t + SUBLANES, d), F32),
                        pltpu.VMEM((nb * t, d), F32),
                        pltpu.VMEM((nb * t, d), BF16),
                        pltpu.VMEM((d // LANES, nb * sp, LANES), F32),
                        pltpu.VMEM((d // LANES, nb * sp, LANES), F32),
                        pltpu.VMEM((d // LANES, nb, LANES), F32)],
        compiler_params=_params("arbitrary"),
        name="rglru",
    )(x3, wx, wg, wm, conv_w, row(conv_b), w_ri, row(b_r), row(b_i), row(lam))


def _sgu_kernel(x_ref, ya_ref, ym_ref, wu_ref, wv_ref, wg_ref, lg_ref, lb_ref, ws_ref, bias_ref,
                wo_ref, og_ref, ob_ref, o_ref, v_sc, m_sc, *, rows):
    x = x_ref[...]
    xb = x.astype(BF16)
    sv = jnp.dot(xb, wv_ref[...], preferred_element_type=F32)
    v_sc[...] = _layer_norm(_gelu(sv), lg_ref[...], lb_ref[...]).astype(BF16)
    tri = (lax.broadcasted_iota(jnp.int32, (SG_CHUNK, SG_CHUNK), 1)
           <= lax.broadcasted_iota(jnp.int32, (SG_CHUNK, SG_CHUNK), 0))
    pair = 2 * LANES
    for gp in range(SG_GROUPS // 2):
        pcols = slice(gp * pair, (gp + 1) * pair)
        u = _gelu_mul_sigmoid(jnp.dot(xb, wu_ref[:, pcols], preferred_element_type=F32),
                              jnp.dot(xb, wg_ref[:, pcols], preferred_element_type=F32))
        for gl in range(2):
            g = 2 * gp + gl
            w = jnp.where(tri, ws_ref[g], 0.0).astype(BF16)
            cols = slice(g * LANES, (g + 1) * LANES)
            for c in range(rows // SG_CHUNK):
                rs = slice(c * SG_CHUNK, (c + 1) * SG_CHUNK)
                mixed = (jnp.dot(w, v_sc[rs, cols], preferred_element_type=F32)
                         + bias_ref[:, cols])
                merged = (u[rs, gl * LANES:(gl + 1) * LANES] * mixed
                          + ya_ref[rs, cols].astype(F32) + ym_ref[rs, cols].astype(F32))
                m_sc[rs, cols] = merged.astype(BF16)
    mix = jnp.dot(m_sc[...], wo_ref[...], preferred_element_type=F32)
    o_ref[...] = _layer_norm(DEEPNORM_ALPHA * x + mix, og_ref[...], ob_ref[...])


def _sgu_merge(x2, ya, ym, wu, wv, wg, ln_g, ln_b, w_s, b_s, w_out, out_g, out_b, rows=1024):
    n, d = x2.shape
    bias = jnp.repeat(b_s.T, d // SG_GROUPS, axis=1)
    rowblk = pl.BlockSpec((rows, d), lambda r: (r, 0))
    full = lambda shape: pl.BlockSpec(shape, lambda r: (0,) * len(shape),
                                      pipeline_mode=pl.Buffered(1))
    return pl.pallas_call(
        functools.partial(_sgu_kernel, rows=rows),
        grid=(n // rows,),
        in_specs=[rowblk, rowblk, rowblk,
                  full((d, d)), full((d, d)), full((d, d)),
                  full((1, d)), full((1, d)),
                  full((SG_GROUPS, SG_CHUNK, SG_CHUNK)), full((SG_CHUNK, d)),
                  full((d, d)), full((1, d)), full((1, d))],
        out_specs=rowblk,
        out_shape=jax.ShapeDtypeStruct((n, d), F32),
        scratch_shapes=[pltpu.VMEM((rows, d), BF16),
                        pltpu.VMEM((rows, d), BF16)],
        compiler_params=_params("parallel"),
        name="sgu_merge",
    )(x2, ya, ym, wu, wv, wg, ln_g.reshape(1, d), ln_b.reshape(1, d), w_s, bias,
      w_out, out_g.reshape(1, d), out_b.reshape(1, d))


def _moba_kernel(q_ref, k_ref, v_ref, gm_ref, o_ref, kaug, vaug, gate_sc, s_sc, p_sc, mx_sc, *, seq):
    blk = MOBA_BLOCK
    nblk = seq // blk
    c2 = (HEAD_DIM ** -0.5) * math.log2(math.e)
    contract_last = (((1,), (1,)), ((), ()))

    n_sel = min(MOBA_TOPK, nblk)
    ind_rows = -(-nblk // BF16_ROWS) * BF16_ROWS
    row = lax.broadcasted_iota(jnp.int32, (ind_rows, seq), 0)
    pos = lax.broadcasted_iota(jnp.int32, (ind_rows, seq), 1)
    ind = jnp.where((pos >= row * blk) & (pos < (row + 1) * blk), 1.0, 0.0).astype(BF16)
    kmean = jnp.dot(ind, k_ref[...], preferred_element_type=F32) * (1.0 / blk)
    kmean = jnp.concatenate([kmean, jnp.zeros((LANES - ind_rows, HEAD_DIM), F32)], axis=0).astype(BF16)
    first_ranked = (n_sel + 1) * blk
    if first_ranked < seq:
        gate_sc[first_ranked:, :] = lax.dot_general(q_ref[first_ranked:, :], kmean, contract_last,
                                                    preferred_element_type=F32)
    kpos_all = lax.broadcasted_iota(jnp.int32, (seq, LANES), 0)
    kblk_all = lax.broadcasted_iota(jnp.int32, (seq, LANES), 1) * blk
    kaug[:, 0:HEAD_DIM] = k_ref[...]
    kaug[:, HEAD_DIM:] = jnp.where((kpos_all >= kblk_all) & (kpos_all < kblk_all + blk),
                                   1.0, 0.0).astype(BF16)
    vaug[:, 0:HEAD_DIM] = v_ref[...]
    vaug[:, HEAD_DIM:] = jnp.ones((seq, HEAD_DIM), BF16)

    lane = lax.broadcasted_iota(jnp.int32, (blk, LANES), 1)
    qpos = lax.broadcasted_iota(jnp.int32, (blk, blk), 0)
    kpos = lax.broadcasted_iota(jnp.int32, (blk, blk), 1)

    def halves_max(s):
        return jnp.maximum(s[:, 0:LANES], s[:, LANES:2 * LANES])

    def scores(qi):
        buf = qi % 2
        rows = slice(qi * blk, (qi + 1) * blk)
        q = q_ref[rows, :]
        s_own = lax.dot_general(q, k_ref[rows, :], contract_last, preferred_element_type=F32)
        s_own = jnp.where(kpos <= qpos, s_own, NEG)
        s_sc[buf, :, rows] = s_own
        macc = halves_max(s_own)
        if qi > 0:
            valid = lane < qi
            if qi > n_sel:
                g = jnp.where(valid, gate_sc[rows, :], -jnp.inf)
                sel = jnp.zeros((blk, LANES), jnp.int32)
                for _ in range(n_sel):
                    gm = jnp.max(g, axis=-1, keepdims=True)
                    idx = jnp.min(jnp.where(g == gm, lane, LANES), axis=-1, keepdims=True)
                    pick = lane == idx
                    sel = jnp.where(pick, 1, sel)
                    g = jnp.where(pick, -jnp.inf, g)
                attended = (sel > 0) & valid
            else:
                attended = valid
            q_aug = jnp.concatenate([q, jnp.where(attended, 0.0, NEG).astype(BF16)], axis=-1)
            for j in range(qi):
                cols = slice(j * blk, (j + 1) * blk)
                sj = lax.dot_general(q_aug, kaug[cols, :], contract_last,
                                     preferred_element_type=F32)
                s_sc[buf, :, cols] = sj
                macc = jnp.maximum(macc, halves_max(sj))
        mx_sc[buf] = macc

    def outputs(qi):
        buf = qi % 2
        rows = slice(qi * blk, (qi + 1) * blk)
        m = jnp.max(mx_sc[buf], axis=-1, keepdims=True)
        for j in range(qi + 1):
            cols = slice(j * blk, (j + 1) * blk)
            p_sc[buf, :, cols] = jnp.exp2((s_sc[buf, :, cols] - m) * c2).astype(BF16)
        acc = jnp.dot(p_sc[buf, :, 0:(qi + 1) * blk], vaug[0:(qi + 1) * blk, :],
                      preferred_element_type=F32)
        y = acc[:, 0:HEAD_DIM] / acc[:, HEAD_DIM:]
        o_ref[rows, :] = (y * _sigmoid(gm_ref[rows, :].astype(F32))).astype(o_ref.dtype)

    scores(0)
    for qi in range(nblk):
        if qi + 1 < nblk:
            scores(qi + 1)
        outputs(qi)


def _moba(proj2, nb, seq, d):
    n = proj2.shape[0]
    hpd = d // HEAD_DIM
    col = lambda off: (lambda b, h: (b, off * hpd + h))
    return pl.pallas_call(
        functools.partial(_moba_kernel, seq=seq),
        grid=(nb, ATT_HEADS),
        in_specs=[pl.BlockSpec((seq, HEAD_DIM), col(0)),
                  pl.BlockSpec((seq, HEAD_DIM), col(1)),
                  pl.BlockSpec((seq, HEAD_DIM), col(2)),
                  pl.BlockSpec((seq, HEAD_DIM), col(3))],
        out_specs=pl.BlockSpec((seq, HEAD_DIM), lambda b, h: (b, h)),
        out_shape=jax.ShapeDtypeStruct((n, d), BF16),
        scratch_shapes=[pltpu.VMEM((seq, HEAD_DIM + LANES), BF16),
                        pltpu.VMEM((seq, 2 * HEAD_DIM), BF16),
                        pltpu.VMEM((seq, LANES), F32),
                        pltpu.VMEM((2, MOBA_BLOCK, seq), F32),
                        pltpu.VMEM((2, MOBA_BLOCK, seq), BF16),
                        pltpu.VMEM((2, MOBA_BLOCK, LANES), F32)],
        compiler_params=_params("parallel", "parallel"),
        name="moba",
    )(proj2, proj2, proj2, proj2)


def _ffn_kernel(x_ref, xh_ref, wup_ref, cw_ref, cb_ref, wdn_ref, g_ref, b_ref, o_ref, act_sc,
                *, tm, tf, d_ff, seq):
    i = pl.program_id(0)
    halo = SUBLANES
    x = x_ref[...]
    xb = x.astype(BF16)
    seq_start = (i * tm) % seq == 0
    xh = jnp.where(seq_start, 0.0, xh_ref[...]).astype(BF16)
    for c in range(d_ff // tf):
        cols = slice(c * tf, (c + 1) * tf)
        wg = wup_ref[:, cols]
        hg = jnp.dot(xb, wg, preferred_element_type=F32)
        hh = jnp.dot(xh, wg, preferred_element_type=F32)
        hu = jnp.dot(xb, wup_ref[:, d_ff + c * tf:d_ff + (c + 1) * tf],
                     preferred_element_type=F32)
        ext = jnp.concatenate([hh, hg], axis=0)
        cw = cw_ref[:, cols]
        gate = cw[2:3] * hg + cb_ref[:, cols]
        for j in range(1, FFN_CONV):
            gate = gate + cw[2 - j:3 - j] * ext[halo - j:halo - j + tm, :]
        act_sc[:, cols] = (_gelu(gate) * hu).astype(BF16)
    acc = jnp.dot(act_sc[...], wdn_ref[...], preferred_element_type=F32)
    o_ref[...] = _layer_norm(DEEPNORM_ALPHA * x + acc, g_ref[...], b_ref[...])


def _ffn(x2, w_up_bf, conv_w, conv_b, w_dn_bf, g, b, seq, tm=1024, tf=256):
    n, d = x2.shape
    d_ff = w_dn_bf.shape[0]
    hb = tm // SUBLANES
    full = lambda shape: pl.BlockSpec(shape, lambda i: (0,) * len(shape),
                                      pipeline_mode=pl.Buffered(1))
    return pl.pallas_call(
        functools.partial(_ffn_kernel, tm=tm, tf=tf, d_ff=d_ff, seq=seq),
        grid=(n // tm,),
        in_specs=[pl.BlockSpec((tm, d), lambda i: (i, 0)),
                  pl.BlockSpec((SUBLANES, d), lambda i: (jnp.maximum(i * hb - 1, 0), 0)),
                  full((d, 2 * d_ff)), full((FFN_CONV, d_ff)), full((1, d_ff)),
                  full((d_ff, d)), full((1, d)), full((1, d))],
        out_specs=pl.BlockSpec((tm, d), lambda i: (i, 0)),
        out_shape=jax.ShapeDtypeStruct((n, d), F32),
        scratch_shapes=[pltpu.VMEM((tm, d_ff), BF16)],
        compiler_params=_params("parallel"),
        name="ffn",
    )(x2, x2, w_up_bf, conv_w, conv_b.reshape(1, d_ff), w_dn_bf, g.reshape(1, d), b.reshape(1, d))


def kernel(x, w_in, conv_rg_w, conv_rg_b, w_rgate, b_rgate, w_igate, b_igate, lru_lambda, sgu_ln_g, sgu_ln_b, w_spatial, b_spatial, w_out, ln_mix_g, ln_mix_b, w_ffn_up, conv_ffn_w, conv_ffn_b, w_ffn_down, ln_ffn_g, ln_ffn_b):
    nb, seq, d = x.shape
    n = nb * seq
    depth = w_in.shape[0]
    x2 = x.reshape(n, d)
    for l in range(depth):
        wcol = lambda g: w_in[l][:, g * d:(g + 1) * d].astype(BF16)
        wcol_half = lambda g: (0.5 * w_in[l][:, g * d:(g + 1) * d]).astype(BF16)
        ya = _rglru(x2.reshape(nb, seq, d), wcol(0), wcol(1), wcol_half(7), conv_rg_w[l], conv_rg_b[l],
                    w_rgate[l], b_rgate[l], w_igate[l], b_igate[l], lru_lambda[l]).reshape(n, d)
        w_att = jnp.concatenate([w_in[l][:, 4 * d:7 * d], w_in[l][:, 9 * d:10 * d]], axis=1).astype(BF16)
        ym = _moba(_in_proj(x2, w_att), nb, seq, d)
        x2 = _sgu_merge(x2, ya, ym, wcol(2), wcol(3), wcol_half(8), sgu_ln_g[l], sgu_ln_b[l], w_spatial[l],
                        b_spatial[l], w_out[l].astype(BF16), ln_mix_g[l], ln_mix_b[l])
        x2 = _ffn(x2, w_ffn_up[l].astype(BF16), conv_ffn_w[l], conv_ffn_b[l],
                  w_ffn_down[l].astype(BF16), ln_ffn_g[l], ln_ffn_b[l], seq)
    return x2.reshape(nb, seq, d)
```

```python
import functools
import math

import jax
import jax.numpy as jnp
from jax import lax
from jax.experimental import pallas as pl
from jax.experimental.pallas import tpu as pltpu

F32 = jnp.float32
BF16 = jnp.bfloat16

RG_CONV = 4
RG_C = 8.0
SG_GROUPS = 8
SG_CHUNK = 128
ATT_HEADS = 8
HEAD_DIM = 128
MOBA_BLOCK = 256
MOBA_TOPK = 3
NBUF = 4
FFN_CONV = 3
LN_EPS = 1e-5
DEPTH = 2
DEEPNORM_ALPHA = (2 * DEPTH) ** 0.25
NEG = -1e30

LANES = 128
SUBLANES = 8
VMEM_LIMIT = 56 * 1024 * 1024

_SQRT_2_OVER_PI = math.sqrt(2.0 / math.pi)


def _gelu(x):
    return x * (0.5 * (1.0 + jnp.tanh(_SQRT_2_OVER_PI * (x + 0.044715 * (x * x * x)))))


def _sigmoid(x):
    return 0.5 * jnp.tanh(0.5 * x) + 0.5


def _sqrt_nonneg(y):
    return jnp.where(y > 0.0, y * lax.rsqrt(y), 0.0)


def _layer_norm(y, g, b):
    mu = jnp.mean(y, axis=-1, keepdims=True)
    d = y - mu
    var = jnp.mean(d * d, axis=-1, keepdims=True)
    return d * lax.rsqrt(var + LN_EPS) * g + b


def _params(*sem):
    return pltpu.CompilerParams(dimension_semantics=sem, vmem_limit_bytes=VMEM_LIMIT)


def _in_proj_kernel(x_ref, w_ref, o_ref):
    o_ref[...] = jnp.dot(x_ref[...].astype(BF16), w_ref[...],
                         preferred_element_type=F32).astype(o_ref.dtype)


def _in_proj(x2, w_bf, tm=1024, tn=2048):
    n, d = x2.shape
    d_in = w_bf.shape[1]
    return pl.pallas_call(
        _in_proj_kernel,
        grid=(n // tm, d_in // tn),
        in_specs=[pl.BlockSpec((tm, d), lambda i, j: (i, 0)),
                  pl.BlockSpec((d, tn), lambda i, j: (0, j))],
        out_specs=pl.BlockSpec((tm, tn), lambda i, j: (i, j)),
        out_shape=jax.ShapeDtypeStruct((n, d_in), BF16),
        compiler_params=_params("parallel", "arbitrary"),
        name="in_proj",
    )(x2, w_bf)


def _rglru_kernel(x_ref, wx_ref, wg_ref, wm_ref, cw_ref, cb_ref, wri_ref, br_ref, bi_ref,
                  lam_ref, o_ref, xbuf, xc_sc, g_sc, a_sc, u_sc, hcarry, *, nb, t, cbw, sp):
    i = pl.program_id(0)
    halo = SUBLANES
    d = x_ref.shape[-1]
    nblk = d // LANES

    @pl.when(i == 0)
    def _():
        xbuf[:, 0:halo, :] = jnp.zeros((nb, halo, d), F32)
        hcarry[...] = jnp.zeros_like(hcarry)

    xb = x_ref[...].reshape(nb * t, d).astype(BF16)
    neg_lam = -lam_ref[...]
    softplus = jnp.maximum(neg_lam, 0.0) + jnp.log1p(jnp.exp(-jnp.abs(neg_lam)))
    decay_rate = -RG_C * softplus

    for c in range(d // cbw):
        cols = slice(c * cbw, (c + 1) * cbw)
        ax = jnp.dot(xb, wx_ref[:, cols], preferred_element_type=F32)
        cw = cw_ref[:, cols]
        for b in range(nb):
            x = ax[b * t:(b + 1) * t, :]
            xbuf[b, halo:halo + t, cols] = x
            xc = cw[3:4] * x + cb_ref[:, cols]
            for j in range(1, RG_CONV):
                xc = xc + cw[3 - j:4 - j] * xbuf[b, halo - j:halo - j + t, cols]
            xbuf[b, 0:halo, cols] = xbuf[b, t:t + halo, cols]
            xc_sc[b * t:(b + 1) * t, cols] = xc
        g_sc[:, cols] = (_gelu(jnp.dot(xb, wg_ref[:, cols], preferred_element_type=F32))
                         * _sigmoid(jnp.dot(xb, wm_ref[:, cols], preferred_element_type=F32))
                         ).astype(BF16)

    half_rate = (0.5 * math.log2(math.e)) * decay_rate
    for k in range(nblk):
        lanes = slice(k * LANES, (k + 1) * LANES)
        xc = xc_sc[:, lanes]
        half_logits = jnp.dot(xc.astype(BF16), wri_ref[k].astype(BF16),
                              preferred_element_type=F32)
        t_r = jnp.tanh(half_logits[:, 0:LANES] + br_ref[:, lanes])
        t_i = jnp.tanh(half_logits[:, LANES:] + bi_ref[:, lanes])
        a = jnp.exp2(half_rate[:, lanes] * t_r + half_rate[:, lanes])
        xc_half = 0.5 * xc
        u = _sqrt_nonneg(1.0 - a * a) * (t_i * xc_half + xc_half)
        for b in range(nb):
            a_sc[k, b * sp:b * sp + t, :] = a[b * t:(b + 1) * t, :]
            u_sc[k, b * sp:b * sp + t, :] = u[b * t:(b + 1) * t, :]

    hs = [hcarry[k] for k in range(nblk)]
    for p in range(t):
        rows = pl.ds(p, nb, stride=sp)
        for k in range(nblk):
            hs[k] = a_sc[k, rows, :] * hs[k] + u_sc[k, rows, :]
            u_sc[k, rows, :] = hs[k]
    for k in range(nblk):
        hcarry[k] = hs[k]

    for b in range(nb):
        h = jnp.concatenate([u_sc[k, b * sp:b * sp + t, :] for k in range(nblk)], axis=-1)
        o_ref[b] = (h * g_sc[b * t:(b + 1) * t, :].astype(F32)).astype(o_ref.dtype)


def _rglru(x3, wx, wg, wm, conv_w, conv_b, w_r, b_r, w_i, b_i, lam, t=128, cbw=256):
    nb, s, d = x3.shape
    sp = t + SUBLANES
    row = lambda a: a.reshape(1, d)
    w_ri = 0.5 * jnp.concatenate([w_r, w_i], axis=-1)
    b_r, b_i = 0.5 * b_r, 0.5 * b_i
    full = lambda shape: pl.BlockSpec(shape, lambda i: (0,) * len(shape),
                                      pipeline_mode=pl.Buffered(1))
    return pl.pallas_call(
        functools.partial(_rglru_kernel, nb=nb, t=t, cbw=cbw, sp=sp),
        grid=(s // t,),
        in_specs=[pl.BlockSpec((nb, t, d), lambda i: (0, i, 0)),
                  full((d, d)), full((d, d)), full((d, d)),
                  full((RG_CONV, d)), full((1, d)),
                  full(w_ri.shape), full((1, d)), full((1, d)), full((1, d))],
        out_specs=pl.BlockSpec((nb, t, d), lambda i: (0, i, 0)),
        out_shape=jax.ShapeDtypeStruct((nb, s, d), BF16),
        scratch_shapes=[pltpu.VMEM((nb, t + SUBLANES, d), F32),
                        pltpu.VMEM((nb * t, d), F32),
                        pltpu.VMEM((nb * t, d), BF16),
                        pltpu.VMEM((d // LANES, nb * sp, LANES), F32),
                        pltpu.VMEM((d // LANES, nb * sp, LANES), F32),
                        pltpu.VMEM((d // LANES, nb, LANES), F32)],
        compiler_params=_params("arbitrary"),
        name="rglru",
    )(x3, wx, wg, wm, conv_w, row(conv_b), w_ri, row(b_r), row(b_i), row(lam))


def _sgu_kernel(x_ref, ya_ref, ym_ref, wu_ref, wv_ref, wg_ref, lg_ref, lb_ref, ws_ref, bias_ref,
                wo_ref, og_ref, ob_ref, o_ref, v_sc, m_sc, *, rows):
    x = x_ref[...]
    xb = x.astype(BF16)
    sv = jnp.dot(xb, wv_ref[...], preferred_element_type=F32)
    v_sc[...] = _layer_norm(_gelu(sv), lg_ref[...], lb_ref[...]).astype(BF16)
    tri = (lax.broadcasted_iota(jnp.int32, (SG_CHUNK, SG_CHUNK), 1)
           <= lax.broadcasted_iota(jnp.int32, (SG_CHUNK, SG_CHUNK), 0))
    pair = 2 * LANES
    for gp in range(SG_GROUPS // 2):
        pcols = slice(gp * pair, (gp + 1) * pair)
        u = (_gelu(jnp.dot(xb, wu_ref[:, pcols], preferred_element_type=F32))
             * _sigmoid(jnp.dot(xb, wg_ref[:, pcols], preferred_element_type=F32)))
        for gl in range(2):
            g = 2 * gp + gl
            w = jnp.where(tri, ws_ref[g], 0.0).astype(BF16)
            cols = slice(g * LANES, (g + 1) * LANES)
            for c in range(rows // SG_CHUNK):
                rs = slice(c * SG_CHUNK, (c + 1) * SG_CHUNK)
                mixed = (jnp.dot(w, v_sc[rs, cols], preferred_element_type=F32)
                         + bias_ref[:, cols])
                merged = (u[rs, gl * LANES:(gl + 1) * LANES] * mixed
                          + ya_ref[rs, cols].astype(F32) + ym_ref[rs, cols].astype(F32))
                m_sc[rs, cols] = merged.astype(BF16)
    mix = jnp.dot(m_sc[...], wo_ref[...], preferred_element_type=F32)
    o_ref[...] = _layer_norm(DEEPNORM_ALPHA * x + mix, og_ref[...], ob_ref[...])


def _sgu_merge(x2, ya, ym, wu, wv, wg, ln_g, ln_b, w_s, b_s, w_out, out_g, out_b, rows=1024):
    n, d = x2.shape
    bias = jnp.repeat(b_s.T, d // SG_GROUPS, axis=1)
    rowblk = pl.BlockSpec((rows, d), lambda r: (r, 0))
    full = lambda shape: pl.BlockSpec(shape, lambda r: (0,) * len(shape),
                                      pipeline_mode=pl.Buffered(1))
    return pl.pallas_call(
        functools.partial(_sgu_kernel, rows=rows),
        grid=(n // rows,),
        in_specs=[rowblk, rowblk, rowblk,
                  full((d, d)), full((d, d)), full((d, d)),
                  full((1, d)), full((1, d)),
                  full((SG_GROUPS, SG_CHUNK, SG_CHUNK)), full((SG_CHUNK, d)),
                  full((d, d)), full((1, d)), full((1, d))],
        out_specs=rowblk,
        out_shape=jax.ShapeDtypeStruct((n, d), F32),
        scratch_shapes=[pltpu.VMEM((rows, d), BF16),
                        pltpu.VMEM((rows, d), BF16)],
        compiler_params=_params("parallel"),
        name="sgu_merge",
    )(x2, ya, ym, wu, wv, wg, ln_g.reshape(1, d), ln_b.reshape(1, d), w_s, bias,
      w_out, out_g.reshape(1, d), out_b.reshape(1, d))


def _moba_kernel(q_ref, k_ref, v_ref, gm_ref, o_ref, kaug, vaug, gate_sc, s_sc, p_sc, mx_sc, *, seq):
    blk = MOBA_BLOCK
    nblk = seq // blk
    c2 = (HEAD_DIM ** -0.5) * math.log2(math.e)
    contract_last = (((1,), (1,)), ((), ()))

    row = lax.broadcasted_iota(jnp.int32, (LANES, seq), 0)
    pos = lax.broadcasted_iota(jnp.int32, (LANES, seq), 1)
    ind = jnp.where((pos >= row * blk) & (pos < (row + 1) * blk), 1.0, 0.0).astype(BF16)
    kmean = (jnp.dot(ind, k_ref[...], preferred_element_type=F32) * (1.0 / blk)).astype(BF16)
    gate_sc[...] = lax.dot_general(q_ref[...], kmean, contract_last, preferred_element_type=F32)
    kpos_all = lax.broadcasted_iota(jnp.int32, (seq, LANES), 0)
    kblk_all = lax.broadcasted_iota(jnp.int32, (seq, LANES), 1) * blk
    kaug[:, 0:HEAD_DIM] = k_ref[...]
    kaug[:, HEAD_DIM:] = jnp.where((kpos_all >= kblk_all) & (kpos_all < kblk_all + blk),
                                   1.0, 0.0).astype(BF16)
    vaug[:, 0:HEAD_DIM] = v_ref[...]
    vaug[:, HEAD_DIM:] = jnp.ones((seq, HEAD_DIM), BF16)

    lane = lax.broadcasted_iota(jnp.int32, (blk, LANES), 1)
    qpos = lax.broadcasted_iota(jnp.int32, (blk, blk), 0)
    kpos = lax.broadcasted_iota(jnp.int32, (blk, blk), 1)
    n_sel = min(MOBA_TOPK, nblk)

    def halves_max(s):
        return jnp.maximum(s[:, 0:LANES], s[:, LANES:2 * LANES])

    def own_and_mask(qi):
        buf = qi % NBUF
        rows = slice(qi * blk, (qi + 1) * blk)
        q = q_ref[rows, :]
        s_own = lax.dot_general(q, k_ref[rows, :], contract_last, preferred_element_type=F32)
        s_own = jnp.where(kpos <= qpos, s_own, NEG)
        s_sc[buf, :, rows] = s_own
        macc = halves_max(s_own)
        if qi == 0:
            return macc, None
        valid = lane < qi
        if qi > n_sel:
            g = jnp.where(valid, gate_sc[rows, :], -jnp.inf)
            sel = jnp.zeros((blk, LANES), jnp.int32)
            for _ in range(n_sel):
                gm = jnp.max(g, axis=-1, keepdims=True)
                idx = jnp.min(jnp.where(g == gm, lane, LANES), axis=-1, keepdims=True)
                pick = lane == idx
                sel = jnp.where(pick, 1, sel)
                g = jnp.where(pick, -jnp.inf, g)
            attended = (sel > 0) & valid
        else:
            attended = valid
        return macc, jnp.concatenate([q, jnp.where(attended, 0.0, NEG).astype(BF16)], axis=-1)

    def scores_pair(t):
        qa, qb = 2 * t, 2 * t + 1
        macc_a, aug_a = own_and_mask(qa)
        macc_b, aug_b = own_and_mask(qb)
        if qa > 0:
            aug_ab = jnp.concatenate([aug_a, aug_b], axis=0)
            for j in range(qa):
                cols = slice(j * blk, (j + 1) * blk)
                sj = lax.dot_general(aug_ab, kaug[cols, :], contract_last,
                                     preferred_element_type=F32)
                s_sc[qa % NBUF, :, cols] = sj[0:blk]
                s_sc[qb % NBUF, :, cols] = sj[blk:]
                macc_a = jnp.maximum(macc_a, halves_max(sj[0:blk]))
                macc_b = jnp.maximum(macc_b, halves_max(sj[blk:]))
        cols = slice(qa * blk, (qa + 1) * blk)
        sj = lax.dot_general(aug_b, kaug[cols, :], contract_last, preferred_element_type=F32)
        s_sc[qb % NBUF, :, cols] = sj
        macc_b = jnp.maximum(macc_b, halves_max(sj))
        mx_sc[qa % NBUF] = macc_a
        mx_sc[qb % NBUF] = macc_b

    def outputs(qi):
        buf = qi % NBUF
        rows = slice(qi * blk, (qi + 1) * blk)
        pbuf = qi % 2
        m = jnp.max(mx_sc[buf], axis=-1, keepdims=True)
        for j in range(qi + 1):
            cols = slice(j * blk, (j + 1) * blk)
            p_sc[pbuf, :, cols] = jnp.exp2((s_sc[buf, :, cols] - m) * c2).astype(BF16)
        acc = jnp.dot(p_sc[pbuf, :, 0:(qi + 1) * blk], vaug[0:(qi + 1) * blk, :],
                      preferred_element_type=F32)
        y = acc[:, 0:HEAD_DIM] / acc[:, HEAD_DIM:]
        o_ref[rows, :] = (y * _sigmoid(gm_ref[rows, :].astype(F32))).astype(o_ref.dtype)

    assert nblk % 2 == 0
    scores_pair(0)
    for t in range(nblk // 2):
        if t + 1 < nblk // 2:
            scores_pair(t + 1)
        outputs(2 * t)
        outputs(2 * t + 1)


def _moba(proj2, nb, seq, d):
    n = proj2.shape[0]
    hpd = d // HEAD_DIM
    col = lambda off: (lambda b, h: (b, off * hpd + h))
    return pl.pallas_call(
        functools.partial(_moba_kernel, seq=seq),
        grid=(nb, ATT_HEADS),
        in_specs=[pl.BlockSpec((seq, HEAD_DIM), col(0)),
                  pl.BlockSpec((seq, HEAD_DIM), col(1)),
                  pl.BlockSpec((seq, HEAD_DIM), col(2)),
                  pl.BlockSpec((seq, HEAD_DIM), col(3))],
        out_specs=pl.BlockSpec((seq, HEAD_DIM), lambda b, h: (b, h)),
        out_shape=jax.ShapeDtypeStruct((n, d), BF16),
        scratch_shapes=[pltpu.VMEM((seq, HEAD_DIM + LANES), BF16),
                        pltpu.VMEM((seq, 2 * HEAD_DIM), BF16),
                        pltpu.VMEM((seq, LANES), F32),
                        pltpu.VMEM((NBUF, MOBA_BLOCK, seq), F32),
                        pltpu.VMEM((2, MOBA_BLOCK, seq), BF16),
                        pltpu.VMEM((NBUF, MOBA_BLOCK, LANES), F32)],
        compiler_params=_params("parallel", "parallel"),
        name="moba",
    )(proj2, proj2, proj2, proj2)


def _ffn_kernel(x_ref, xh_ref, wup_ref, cw_ref, cb_ref, wdn_ref, g_ref, b_ref, o_ref, act_sc,
                *, tm, tf, d_ff, seq):
    i = pl.program_id(0)
    halo = SUBLANES
    x = x_ref[...]
    xb = x.astype(BF16)
    seq_start = (i * tm) % seq == 0
    xh = jnp.where(seq_start, 0.0, xh_ref[...]).astype(BF16)
    for c in range(d_ff // tf):
        cols = slice(c * tf, (c + 1) * tf)
        wg = wup_ref[:, cols]
        hg = jnp.dot(xb, wg, preferred_element_type=F32)
        hh = jnp.dot(xh, wg, preferred_element_type=F32)
        hu = jnp.dot(xb, wup_ref[:, d_ff + c * tf:d_ff + (c + 1) * tf],
                     preferred_element_type=F32)
        ext = jnp.concatenate([hh, hg], axis=0)
        cw = cw_ref[:, cols]
        gate = cw[2:3] * hg + cb_ref[:, cols]
        for j in range(1, FFN_CONV):
            gate = gate + cw[2 - j:3 - j] * ext[halo - j:halo - j + tm, :]
        act_sc[:, cols] = (_gelu(gate) * hu).astype(BF16)
    acc = jnp.dot(act_sc[...], wdn_ref[...], preferred_element_type=F32)
    o_ref[...] = _layer_norm(DEEPNORM_ALPHA * x + acc, g_ref[...], b_ref[...])


def _ffn(x2, w_up_bf, conv_w, conv_b, w_dn_bf, g, b, seq, tm=1024, tf=256):
    n, d = x2.shape
    d_ff = w_dn_bf.shape[0]
    hb = tm // SUBLANES
    full = lambda shape: pl.BlockSpec(shape, lambda i: (0,) * len(shape),
                                      pipeline_mode=pl.Buffered(1))
    return pl.pallas_call(
        functools.partial(_ffn_kernel, tm=tm, tf=tf, d_ff=d_ff, seq=seq),
        grid=(n // tm,),
        in_specs=[pl.BlockSpec((tm, d), lambda i: (i, 0)),
                  pl.BlockSpec((SUBLANES, d), lambda i: (jnp.maximum(i * hb - 1, 0), 0)),
                  full((d, 2 * d_ff)), full((FFN_CONV, d_ff)), full((1, d_ff)),
                  full((d_ff, d)), full((1, d)), full((1, d))],
        out_specs=pl.BlockSpec((tm, d), lambda i: (i, 0)),
        out_shape=jax.ShapeDtypeStruct((n, d), F32),
        scratch_shapes=[pltpu.VMEM((tm, d_ff), BF16)],
        compiler_params=_params("parallel"),
        name="ffn",
    )(x2, x2, w_up_bf, conv_w, conv_b.reshape(1, d_ff), w_dn_bf, g.reshape(1, d), b.reshape(1, d))


def kernel(x, w_in, conv_rg_w, conv_rg_b, w_rgate, b_rgate, w_igate, b_igate, lru_lambda, sgu_ln_g, sgu_ln_b, w_spatial, b_spatial, w_out, ln_mix_g, ln_mix_b, w_ffn_up, conv_ffn_w, conv_ffn_b, w_ffn_down, ln_ffn_g, ln_ffn_b):
    nb, seq, d = x.shape
    n = nb * seq
    depth = w_in.shape[0]
    x2 = x.reshape(n, d)
    for l in range(depth):
        wcol = lambda g: w_in[l][:, g * d:(g + 1) * d].astype(BF16)
        ya = _rglru(x2.reshape(nb, seq, d), wcol(0), wcol(1), wcol(7), conv_rg_w[l], conv_rg_b[l],
                    w_rgate[l], b_rgate[l], w_igate[l], b_igate[l], lru_lambda[l]).reshape(n, d)
        w_att = jnp.concatenate([w_in[l][:, 4 * d:7 * d], w_in[l][:, 9 * d:10 * d]], axis=1).astype(BF16)
        ym = _moba(_in_proj(x2, w_att), nb, seq, d)
        x2 = _sgu_merge(x2, ya, ym, wcol(2), wcol(3), wcol(8), sgu_ln_g[l], sgu_ln_b[l], w_spatial[l],
                        b_spatial[l], w_out[l].astype(BF16), ln_mix_g[l], ln_mix_b[l])
        x2 = _ffn(x2, w_ffn_up[l].astype(BF16), conv_ffn_w[l], conv_ffn_b[l],
                  w_ffn_down[l].astype(BF16), ln_ffn_g[l], ln_ffn_b[l], seq)
    return x2.reshape(nb, seq, d)
```
